```python
import math
import jax
import jax.numpy as jnp
from jax import lax
import numpy as np

D_MODEL = 1024
BATCH = 2
SEQ = 8192
DEPTH = 4

A_HEAD_DIM = 64
A_HEADS = D_MODEL // A_HEAD_DIM
A_KV_HEADS = 2
A_GROUP = A_HEADS // A_KV_HEADS
WINDOW = 128
ATTN_BLOCK = 128
ROPE_THETA = 10000.0
R_HEADS = 4
R_KEY_DIM = D_MODEL // 8
R_VAL_DIM = 2 * R_KEY_DIM
RET_CHUNK = 128
A_Q = A_HEADS * A_HEAD_DIM
A_KV = A_KV_HEADS * A_HEAD_DIM
R_QK = R_HEADS * R_KEY_DIM
R_V = R_HEADS * R_VAL_DIM
AB_SPLITS = (A_Q, A_Q + A_KV, A_Q + 2 * A_KV, A_Q + 2 * A_KV + R_QK,
             A_Q + 2 * A_KV + 2 * R_QK, A_Q + 2 * A_KV + 2 * R_QK + R_V)
AB_IN = A_Q + 2 * A_KV + 2 * R_QK + 2 * R_V
AB_OUT = A_Q + R_V
SGU_CHUNK = 128
SGU_GROUPS = 4
SGU_GROUP_DIM = D_MODEL // 4
SGU_WIDTH = SGU_GROUPS * SGU_GROUP_DIM
LRU_HEADS = 5
LRU_BLOCK = 256
LRU_WIDTH = LRU_HEADS * LRU_BLOCK
CONV_WIDTH = 4
LRU_C = 8.0
CD_SPLITS = (SGU_WIDTH, 2 * SGU_WIDTH, 2 * SGU_WIDTH + LRU_WIDTH)
CD_IN = 2 * SGU_WIDTH + 2 * LRU_WIDTH
CD_OUT = SGU_WIDTH + LRU_WIDTH
N_EXPERTS = 128
TOP_K = 8
EXPERT_DIM = 256
SHARED_DIM = 256
ROUTE_SCALE = 2.5
EXPERT_BLOCK = 128
DEEPNORM_ALPHA = (2 * DEPTH) ** 0.25
DEEPNORM_BETA = (8 * DEPTH) ** -0.25
N_AB = (DEPTH + 1) // 2
N_CD = DEPTH // 2
LN_EPS = 1e-5
NEG_INF = -1e30

kernel_name = "hybrid_swa_retention_sgu_rglru_moe_deepnorm"

F32 = jnp.float32


def _standardize(x):
    xf = x.astype(F32)
    xc = xf - jnp.mean(xf, axis=-1, keepdims=True)
    return xc * lax.rsqrt(jnp.mean(xc * xc, axis=-1, keepdims=True) + LN_EPS)


def layer_norm(x, g, b):
    return (_standardize(x) * g.astype(F32) + b.astype(F32)).astype(x.dtype)


def rope(x, inv_freq):
    S = x.shape[1]
    ang = jnp.arange(S, dtype=F32)[:, None] * inv_freq[None, :]
    cos = jnp.cos(ang)[:, None, :]
    sin = jnp.sin(ang)[:, None, :]
    xf = x.astype(F32)
    x1, x2 = jnp.split(xf, 2, axis=-1)
    return jnp.concatenate([x1 * cos - x2 * sin, x2 * cos + x1 * sin], axis=-1).astype(x.dtype)


def sliding_window_attention(q, k, v, sinks):
    Bsz, S, _, d = q.shape
    C = ATTN_BLOCK
    N = S // C
    qb = q.reshape(Bsz, N, C, A_KV_HEADS, A_GROUP, d)
    kb = k.reshape(Bsz, N, C, A_KV_HEADS, d)
    vb = v.reshape(Bsz, N, C, A_KV_HEADS, d)

    def with_prev(t):
        prev = jnp.pad(t[:, :-1], ((0, 0), (1, 0), (0, 0), (0, 0), (0, 0)))
        return jnp.concatenate([prev, t], axis=2)

    kw, vw = with_prev(kb), with_prev(vb)
    s = jnp.einsum('bnqhgd,bnkhd->bnhgqk', qb, kw, preferred_element_type=F32) * (d ** -0.5)
    n_idx = jnp.arange(N)[:, None, None]
    q_idx = jnp.arange(C)[None, :, None]
    k_idx = jnp.arange(2 * C)[None, None, :]
    rel = C + q_idx - k_idx
    valid = (rel >= 0) & (rel < WINDOW) & ((n_idx > 0) | (k_idx >= C))
    s = jnp.where(valid[None, :, None, None], s, NEG_INF)
    sink = jnp.broadcast_to(sinks.astype(F32).reshape(1, 1, A_KV_HEADS, A_GROUP, 1, 1), s.shape[:-1] + (1,))
    p = jax.nn.softmax(jnp.concatenate([s, sink], axis=-1), axis=-1)[..., :-1]
    o = jnp.einsum('bnhgqk,bnkhd->bnqhgd', p.astype(v.dtype), vw)
    return o.reshape(Bsz, S, A_HEADS * d)


def retention(q, k, v):
    Bsz, S, H, dk = q.shape
    dv = v.shape[-1]
    C = RET_CHUNK
    N = S // C
    log_g = jnp.log1p(-jnp.exp2(-5.0 - jnp.arange(H, dtype=F32)))
    qf = q.astype(F32).reshape(Bsz, N, C, H, dk)
    kf = k.astype(F32).reshape(Bsz, N, C, H, dk) * (dk ** -0.5)
    vf = v.astype(F32).reshape(Bsz, N, C, H, dv)
    idx = jnp.arange(C, dtype=F32)
    diff = idx[:, None] - idx[None, :]
    decay = jnp.where(diff >= 0, jnp.exp(log_g[:, None, None] * jnp.maximum(diff, 0.0)), 0.0)
    s = jnp.einsum('bnihd,bnjhd->bnhij', qf, kf) * decay[None, None]
    o_intra = jnp.einsum('bnhij,bnjhe->bnihe', s, vf)
    k_dec = kf * jnp.exp(log_g[None, :] * (C - 1.0 - idx)[:, None])[None, None, :, :, None]
    kv = jnp.einsum('bnjhd,bnjhe->nbhde', k_dec, vf)
    g_chunk = jnp.exp(log_g * C)[None, :, None, None]

    def step(state, kv_n):
        return g_chunk * state + kv_n, state

    _, s_prev = lax.scan(step, jnp.zeros_like(kv[0]), kv)
    q_dec = qf * jnp.exp(log_g[None, :] * (idx + 1.0)[:, None])[None, None, :, :, None]
    o_cross = jnp.einsum('bnihd,nbhde->bnihe', q_dec, s_prev)
    return (o_intra + o_cross).reshape(Bsz, S, H, dv)


def ab_mixer(x, w_in, b_in, sinks, w_out, b_out):
    Bsz, S, _ = x.shape
    z = x @ w_in + b_in
    q, k, v, rq, rk, rv, rg = jnp.split(z, AB_SPLITS, axis=-1)
    attn_freq = ROPE_THETA ** (-jnp.arange(0, A_HEAD_DIM, 2, dtype=F32) / A_HEAD_DIM)
    q = rope(q.reshape(Bsz, S, A_HEADS, A_HEAD_DIM), attn_freq)
    k = rope(k.reshape(Bsz, S, A_KV_HEADS, A_HEAD_DIM), attn_freq)
    v = v.reshape(Bsz, S, A_KV_HEADS, A_HEAD_DIM)
    y_a = sliding_window_attention(q, k, v, sinks)
    ret_freq = ROPE_THETA ** (-jnp.linspace(0.0, 1.0, R_KEY_DIM // 2, dtype=F32))
    rq = rope(rq.reshape(Bsz, S, R_HEADS, R_KEY_DIM), ret_freq)
    rk = rope(rk.reshape(Bsz, S, R_HEADS, R_KEY_DIM), ret_freq)
    ro = retention(rq, rk, rv.reshape(Bsz, S, R_HEADS, R_VAL_DIM))
    ro = _standardize(ro).reshape(Bsz, S, R_V)
    y_b = (jax.nn.silu(rg.astype(F32)) * ro).astype(x.dtype)
    return jnp.concatenate([y_a, y_b], axis=-1) @ w_out + b_out


def spatial_gating(u, v, w_s, b_s):
    Bsz, S, W = v.shape
    N = S // SGU_CHUNK
    vn = _standardize(v).reshape(Bsz, N, SGU_CHUNK, SGU_GROUPS, SGU_GROUP_DIM)
    w = jnp.tril(w_s.astype(F32))
    mixed = jnp.einsum('gij,bnjgd->bnigd', w, vn) + b_s.astype(F32).T[None, None, :, :, None]
    return u * mixed.reshape(Bsz, S, W).astype(u.dtype)


def rg_lru(xr, conv_w, conv_b, wa, ba, wx, bx, lam):
    Bsz, S, W = xr.shape
    xc = lax.conv_general_dilated(xr, conv_w[:, None, :], window_strides=(1,),
                                  padding=[(CONV_WIDTH - 1, 0)],
                                  dimension_numbers=('NWC', 'WIO', 'NWC'),
                                  feature_group_count=W) + conv_b
    xb = xc.reshape(Bsz, S, LRU_HEADS, LRU_BLOCK)
    gate_r = jax.nn.sigmoid((jnp.einsum('bshi,hij->bshj', xb, wa).reshape(Bsz, S, W) + ba).astype(F32))
    gate_i = jax.nn.sigmoid((jnp.einsum('bshi,hij->bshj', xb, wx).reshape(Bsz, S, W) + bx).astype(F32))
    log_a = -LRU_C * gate_r * jax.nn.softplus(-lam.astype(F32))
    a = jnp.exp(log_a)
    b = jnp.sqrt(-jnp.expm1(2.0 * log_a)) * (gate_i * xc.astype(F32))

    def combine(left, right):
        a1, b1 = left
        a2, b2 = right
        return a1 * a2, a2 * b1 + b2

    _, h = lax.associative_scan(combine, (a, b), axis=1)
    return h.astype(xr.dtype)


def cd_mixer(x, w_in, b_in, sgu_w, sgu_b, conv_w, conv_b, wa, ba, wx, bx, lam, w_out, b_out):
    z = x @ w_in + b_in
    u, v, gate, xr = jnp.split(z, CD_SPLITS, axis=-1)
    y_c = spatial_gating(jax.nn.gelu(u), jax.nn.gelu(v), sgu_w, sgu_b)
    y_d = jax.nn.gelu(gate) * rg_lru(xr, conv_w, conv_b, wa, ba, wx, bx, lam)
    return jnp.concatenate([y_c, y_d], axis=-1) @ w_out + b_out


def moe_ffn(x, w_router, b_router, w_gate, w_up, w_down, ws_gate, ws_up, ws_down):
    Bsz, S, D = x.shape
    T = Bsz * S
    TK = T * TOP_K
    xt = x.reshape(T, D)
    scores = jax.nn.sigmoid(jnp.matmul(xt.astype(F32), w_router.astype(F32)))
    _, top_idx = lax.top_k(scores + b_router.astype(F32), TOP_K)
    top_s = jnp.take_along_axis(scores, top_idx, axis=-1)
    gates = top_s / jnp.sum(top_s, axis=-1, keepdims=True) * ROUTE_SCALE
    flat_e = top_idx.reshape(-1)
    flat_t = jnp.repeat(jnp.arange(T, dtype=jnp.int32), TOP_K)
    flat_g = gates.reshape(-1)
    order = jnp.argsort(flat_e)
    se, st, sg = flat_e[order], flat_t[order], flat_g[order]
    counts = jnp.bincount(flat_e, length=N_EXPERTS)
    padded = (counts + EXPERT_BLOCK - 1) // EXPERT_BLOCK * EXPERT_BLOCK
    start = jnp.cumsum(counts) - counts
    padded_end = jnp.cumsum(padded)
    padded_start = padded_end - padded
    dest = padded_start[se] + jnp.arange(TK, dtype=jnp.int32) - start[se]
    n_rows = -(-TK // EXPERT_BLOCK) * EXPERT_BLOCK + N_EXPERTS * EXPERT_BLOCK
    n_blocks = n_rows // EXPERT_BLOCK
    row_tok = jnp.zeros((n_rows,), jnp.int32).at[dest].set(st)
    row_gate = jnp.zeros((n_rows,), F32).at[dest].set(sg)
    block_e = jnp.minimum(jnp.searchsorted(padded_end, jnp.arange(n_blocks) * EXPERT_BLOCK, side='right'),
                          N_EXPERTS - 1)

    def expert_block(acc, blk):
        tok, g, e = blk
        xb = xt[tok]
        h = jax.nn.silu(xb @ w_gate[e]) * (xb @ w_up[e])
        y = (h @ w_down[e]).astype(F32) * g[:, None]
        return acc.at[tok].add(y), None

    routed, _ = lax.scan(expert_block, jnp.zeros((T, D), F32),
                         (row_tok.reshape(n_blocks, EXPERT_BLOCK), row_gate.reshape(n_blocks, EXPERT_BLOCK), block_e))
    shared = (jax.nn.silu(xt @ ws_gate) * (xt @ ws_up)) @ ws_down
    return (routed.astype(x.dtype) + shared).reshape(Bsz, S, D)


def setup_inputs(seed: int = 0) -> dict:
    key = jax.random.key(seed)
    ks = iter(jax.random.split(key, 40))

    def nrm(shape, scale):
        return jax.random.normal(next(ks), shape, F32) * scale

    D = D_MODEL
    u = jax.random.uniform(next(ks), (N_CD, LRU_WIDTH), F32, minval=0.9, maxval=0.999)
    a0 = u ** (1.0 / LRU_C)
    lru_lambda = jnp.log(a0) - jnp.log1p(-a0)
    return {
        "x": nrm((BATCH, SEQ, D), 1.0),
        "ab_w_in": nrm((N_AB, D, AB_IN), D ** -0.5),
        "ab_b_in": nrm((N_AB, AB_IN), 0.02),
        "ab_sinks": nrm((N_AB, A_HEADS), 1.0),
        "ab_w_out": nrm((N_AB, AB_OUT, D), AB_OUT ** -0.5 * DEEPNORM_BETA),
        "ab_b_out": nrm((N_AB, D), 0.02),
        "cd_w_in": nrm((N_CD, D, CD_IN), D ** -0.5),
        "cd_b_in": nrm((N_CD, CD_IN), 0.02),
        "sgu_w": nrm((N_CD, SGU_GROUPS, SGU_CHUNK, SGU_CHUNK), SGU_CHUNK ** -0.5),
        "sgu_b": 1.0 + nrm((N_CD, SGU_GROUPS, SGU_CHUNK), 0.1),
        "conv_w": nrm((N_CD, CONV_WIDTH, LRU_WIDTH), CONV_WIDTH ** -0.5),
        "conv_b": nrm((N_CD, LRU_WIDTH), 0.02),
        "lru_wa": nrm((N_CD, LRU_HEADS, LRU_BLOCK, LRU_BLOCK), LRU_BLOCK ** -0.5),
        "lru_ba": nrm((N_CD, LRU_WIDTH), 0.02),
        "lru_wx": nrm((N_CD, LRU_HEADS, LRU_BLOCK, LRU_BLOCK), LRU_BLOCK ** -0.5),
        "lru_bx": nrm((N_CD, LRU_WIDTH), 0.02),
        "lru_lambda": lru_lambda,
        "cd_w_out": nrm((N_CD, CD_OUT, D), CD_OUT ** -0.5 * DEEPNORM_BETA),
        "cd_b_out": nrm((N_CD, D), 0.02),
        "ln_mix_g": 1.0 + nrm((DEPTH, D), 0.02),
        "ln_mix_b": nrm((DEPTH, D), 0.02),
        "router_w": nrm((DEPTH, D, N_EXPERTS), D ** -0.5),
        "router_b": nrm((DEPTH, N_EXPERTS), 0.01),
        "exp_w_gate": nrm((DEPTH, N_EXPERTS, D, EXPERT_DIM), D ** -0.5),
        "exp_w_up": nrm((DEPTH, N_EXPERTS, D, EXPERT_DIM), D ** -0.5),
        "exp_w_down": nrm((DEPTH, N_EXPERTS, EXPERT_DIM, D), EXPERT_DIM ** -0.5 * DEEPNORM_BETA),
        "shared_w_gate": nrm((DEPTH, D, SHARED_DIM), D ** -0.5),
        "shared_w_up": nrm((DEPTH, D, SHARED_DIM), D ** -0.5),
        "shared_w_down": nrm((DEPTH, SHARED_DIM, D), SHARED_DIM ** -0.5 * DEEPNORM_BETA),
        "ln_ffn_g": 1.0 + nrm((DEPTH, D), 0.02),
        "ln_ffn_b": nrm((DEPTH, D), 0.02),
    }


def reference(x, ab_w_in, ab_b_in, ab_sinks, ab_w_out, ab_b_out,
              cd_w_in, cd_b_in, sgu_w, sgu_b, conv_w, conv_b,
              lru_wa, lru_ba, lru_wx, lru_bx, lru_lambda, cd_w_out, cd_b_out,
              ln_mix_g, ln_mix_b, router_w, router_b, exp_w_gate, exp_w_up, exp_w_down,
              shared_w_gate, shared_w_up, shared_w_down, ln_ffn_g, ln_ffn_b):
    for layer in range(DEPTH):
        j = layer // 2
        if layer % 2 == 0:
            h = ab_mixer(x, ab_w_in[j], ab_b_in[j], ab_sinks[j], ab_w_out[j], ab_b_out[j])
        else:
            h = cd_mixer(x, cd_w_in[j], cd_b_in[j], sgu_w[j], sgu_b[j], conv_w[j], conv_b[j],
                         lru_wa[j], lru_ba[j], lru_wx[j], lru_bx[j], lru_lambda[j],
                         cd_w_out[j], cd_b_out[j])
        x = layer_norm(DEEPNORM_ALPHA * x + h, ln_mix_g[layer], ln_mix_b[layer])
        f = moe_ffn(x, router_w[layer], router_b[layer], exp_w_gate[layer], exp_w_up[layer],
                    exp_w_down[layer], shared_w_gate[layer], shared_w_up[layer], shared_w_down[layer])
        x = layer_norm(DEEPNORM_ALPHA * x + f, ln_ffn_g[layer], ln_ffn_b[layer])
    return x
```

```python
import functools
import math

import jax
import jax.numpy as jnp
from jax import lax
from jax.experimental import pallas as pl
from jax.experimental.pallas import tpu as pltpu

F32 = jnp.float32
BF16 = jnp.bfloat16
I32 = jnp.int32

CHUNK = 128
A_HEADS, A_KV_HEADS, A_HEAD_DIM = 16, 2, 64
R_HEADS, R_KEY_DIM, R_VAL_DIM = 4, 128, 256
SGU_GROUPS, SGU_GROUP_DIM = 4, 256
LRU_HEADS, LRU_BLOCK = 5, 256
CONV_WIDTH = 4
LRU_C = 8.0
TOP_K = 8
ROUTE_SCALE = 2.5
ROPE_THETA = 10000.0
LN_EPS = 1e-5
NEG_INF = -1e30
DEPTH = 4
DEEPNORM_ALPHA = (2 * DEPTH) ** 0.25

LANES = 128
VMEM_LIMIT_BYTES = 48 * 1024 * 1024
ROW_TILE = 256
ROUTER_TILE = 512
EXPERT_ROWS = 256
COMBINE_TILE = 128


def _cparams(*sem):
    return pltpu.CompilerParams(dimension_semantics=sem, vmem_limit_bytes=VMEM_LIMIT_BYTES)


def _standardize(x):
    xc = x - jnp.mean(x, axis=-1, keepdims=True)
    return xc * lax.rsqrt(jnp.mean(xc * xc, axis=-1, keepdims=True) + LN_EPS)


def _dot(a, b):
    return jnp.dot(a, b, preferred_element_type=F32)


def _dot_nt(a, b):
    return lax.dot_general(a, b, (((1,), (1,)), ((), ())), preferred_element_type=F32)


def _inproj_body(x_ref, w_ref, b_ref, o_ref):
    o_ref[...] = _dot(x_ref[...].astype(BF16), w_ref[...]) + b_ref[...]


def _inproj(x2d, w_bf16, b_row):
    T, D = x2d.shape
    N = w_bf16.shape[1]
    return pl.pallas_call(
        _inproj_body,
        grid=(T // ROW_TILE,),
        in_specs=[pl.BlockSpec((ROW_TILE, D), lambda i: (i, 0)),
                  pl.BlockSpec((D, N), lambda i: (0, 0)),
                  pl.BlockSpec((1, N), lambda i: (0, 0))],
        out_specs=pl.BlockSpec((ROW_TILE, N), lambda i: (i, 0)),
        out_shape=jax.ShapeDtypeStruct((T, N), F32),
        compiler_params=_cparams("parallel"),
        name="inproj",
    )(x2d, w_bf16, b_row)


def _ab_body(sinks_ref, gch_ref, z_ref, cosa_ref, sina_ref, cosr_ref, sinr_ref, decay_ref, qdec_ref, kdec_ref,
             o_ref, kprev_ref, vprev_ref, state_ref):
    n = pl.program_id(1)
    C = CHUNK

    @pl.when(n == 0)
    def _():
        kprev_ref[...] = jnp.zeros_like(kprev_ref)
        vprev_ref[...] = jnp.zeros_like(vprev_ref)
        state_ref[...] = jnp.zeros_like(state_ref)

    lane = lax.broadcasted_iota(I32, (C, LANES), 1)
    lo_half = lane < 64
    first_32 = (lane & 32) == 0
    cosa = cosa_ref[...]
    sina = sina_ref[...]

    def rope64(x):
        swapped = jnp.where(first_32, pltpu.roll(x, 96, 1), pltpu.roll(x, 32, 1))
        return x * cosa + swapped * sina

    k = rope64(z_ref[0, :, 1024:1152])
    v = z_ref[0, :, 1152:1280]
    kw = jnp.concatenate([kprev_ref[...], k], axis=0)
    vw = jnp.concatenate([vprev_ref[...], v], axis=0)
    kprev_ref[...] = k
    vprev_ref[...] = v
    kw_sw = pltpu.roll(kw, 64, 1)
    vw_sw = pltpu.roll(vw, 64, 1)
    lane2 = lax.broadcasted_iota(I32, (2 * C, LANES), 1)
    lo2 = lane2 < 64

    rows = 8 * C
    qi = lax.broadcasted_iota(I32, (rows, 2 * C), 0) & (C - 1)
    kj = lax.broadcasted_iota(I32, (rows, 2 * C), 1)
    valid = (kj > qi) & (kj <= qi + C) & ((n > 0) | (kj >= C))
    rblk = lax.broadcasted_iota(I32, (rows, 1), 0) // C

    for g in range(A_KV_HEADS):
        if g == 0:
            kdup = jnp.where(lo2, kw, kw_sw)
            vlo = jnp.where(lo2, vw, 0.0)
            vhi = jnp.where(lo2, 0.0, vw_sw)
        else:
            kdup = jnp.where(lo2, kw_sw, kw)
            vlo = jnp.where(lo2, vw_sw, 0.0)
            vhi = jnp.where(lo2, 0.0, vw)
        qs = [rope64(z_ref[0, :, c * LANES:(c + 1) * LANES]) * (A_HEAD_DIM ** -0.5)
              for c in range(4 * g, 4 * g + 4)]
        qg = jnp.concatenate([jnp.where(lo_half, qc, 0.0) for qc in qs]
                             + [jnp.where(lo_half, 0.0, qc) for qc in qs], axis=0).astype(BF16)
        s = _dot_nt(qg, kdup.astype(BF16))
        s = jnp.where(valid, s, NEG_INF)
        sink = jnp.zeros((rows, 1), F32)
        for j in range(8):
            head = 8 * g + (2 * j if j < 4 else 2 * (j - 4) + 1)
            sink = jnp.where(rblk == j, sinks_ref[head], sink)
        m = jnp.maximum(jnp.max(s, axis=-1, keepdims=True), sink)
        e = jnp.exp(s - m)
        den = jnp.sum(e, axis=-1, keepdims=True) + jnp.exp(sink - m)
        p = (e * (1.0 / den)).astype(BF16)
        p2 = jnp.concatenate([p[:4 * C], p[4 * C:]], axis=1)
        vblk = jnp.concatenate([vlo, vhi], axis=0).astype(BF16)
        o = _dot(p2, vblk)
        for c in range(4):
            col = (4 * g + c) * LANES
            o_ref[0, :, col:col + LANES] = o[c * C:(c + 1) * C].astype(o_ref.dtype)

    cosr = cosr_ref[...]
    sinr = sinr_ref[...]

    def rope128(x):
        return x * cosr + pltpu.roll(x, 64, 1) * sinr

    q0, k0, v0, g0 = 1280, 1792, 2304, 3328
    for h in range(R_HEADS):
        qh = rope128(z_ref[0, :, q0 + h * 128:q0 + (h + 1) * 128])
        kh = rope128(z_ref[0, :, k0 + h * 128:k0 + (h + 1) * 128]) * (R_KEY_DIM ** -0.5)
        vb = z_ref[0, :, v0 + h * 256:v0 + (h + 1) * 256].astype(BF16)
        gh = z_ref[0, :, g0 + h * 256:g0 + (h + 1) * 256]
        s = _dot_nt(qh.astype(BF16), kh.astype(BF16)) * decay_ref[h]
        st = state_ref[h]
        o = _dot(s.astype(BF16), vb) + _dot((qh * qdec_ref[h]).astype(BF16), st.astype(BF16))
        kd_t = jnp.transpose(kh * kdec_ref[h]).astype(BF16)
        state_ref[h] = gch_ref[h] * st + _dot(kd_t, vb)
        y = jax.nn.silu(gh) * _standardize(o)
        col = 1024 + h * 256
        o_ref[0, :, col:col + 256] = y.astype(o_ref.dtype)


def _ab_core(z, sinks, tables):
    B, S, W = z.shape
    cosa, sina, cosr, sinr, decay, qdec, kdec, gch = tables
    smem = pl.BlockSpec(memory_space=pltpu.SMEM)
    tab = pl.BlockSpec((CHUNK, LANES), lambda b, n: (n, 0))
    const3 = pl.BlockSpec((R_HEADS, CHUNK, CHUNK), lambda b, n: (0, 0, 0))
    return pl.pallas_call(
        _ab_body,
        grid=(B, S // CHUNK),
        in_specs=[smem, smem,
                  pl.BlockSpec((1, CHUNK, W), lambda b, n: (b, n, 0)),
                  tab, tab, tab, tab, const3, const3, const3],
        out_specs=pl.BlockSpec((1, CHUNK, 2048), lambda b, n: (b, n, 0)),
        out_shape=jax.ShapeDtypeStruct((B, S, 2048), BF16),
        scratch_shapes=[pltpu.VMEM((CHUNK, LANES), F32), pltpu.VMEM((CHUNK, LANES), F32),
                        pltpu.VMEM((R_HEADS, R_KEY_DIM, R_VAL_DIM), F32)],
        compiler_params=_cparams("parallel", "arbitrary"),
        name="ab_core",
    )(sinks, gch, z, cosa, sina, cosr, sinr, decay, qdec, kdec)


def _cd_body(z_ref, sw_ref, sbt_ref, cw_ref, cb_ref, wa_ref, ba_ref, wx_ref, bx_ref, lam_ref,
             o_ref, xwin_ref, hc_ref):
    n = pl.program_id(1)
    C = CHUNK
    WL = LRU_HEADS * LRU_BLOCK

    @pl.when(n == 0)
    def _():
        xwin_ref[0:8, :] = jnp.zeros((8, WL), F32)
        hc_ref[...] = jnp.zeros_like(hc_ref)

    gu = jax.nn.gelu(z_ref[0, :, 0:1024])
    vn = _standardize(jax.nn.gelu(z_ref[0, :, 1024:2048])).astype(BF16)
    ri = lax.broadcasted_iota(I32, (C, C), 0)
    ci = lax.broadcasted_iota(I32, (C, C), 1)
    for g in range(SGU_GROUPS):
        w = jnp.where(ci <= ri, sw_ref[g], 0.0).astype(BF16)
        col = g * SGU_GROUP_DIM
        mixed = _dot(w, vn[:, col:col + SGU_GROUP_DIM]) + sbt_ref[:, g:g + 1]
        o_ref[0, :, col:col + SGU_GROUP_DIM] = (gu[:, col:col + SGU_GROUP_DIM] * mixed).astype(o_ref.dtype)

    xr = z_ref[0, :, 3328:3328 + WL]
    xwin_ref[8:8 + C, :] = xr
    xc = (cw_ref[3:4, :] * xr + cw_ref[2:3, :] * xwin_ref[7:7 + C, :] + cw_ref[1:2, :] * xwin_ref[6:6 + C, :]
          + cw_ref[0:1, :] * xwin_ref[5:5 + C, :] + cb_ref[...])
    xwin_ref[0:8, :] = xr[C - 8:C]
    xcb = xc.astype(BF16)
    ga, gi = [], []
    for h in range(LRU_HEADS):
        xb = xcb[:, h * LRU_BLOCK:(h + 1) * LRU_BLOCK]
        ga.append(_dot(xb, wa_ref[h]))
        gi.append(_dot(xb, wx_ref[h]))
    gate_r = jax.nn.sigmoid(jnp.concatenate(ga, axis=1) + ba_ref[...])
    gate_i = jax.nn.sigmoid(jnp.concatenate(gi, axis=1) + bx_ref[...])
    nl = -lam_ref[...]
    softplus = jnp.maximum(nl, 0.0) + jnp.log1p(jnp.exp(-jnp.abs(nl)))
    log_a = -LRU_C * gate_r * softplus
    a = jnp.exp(log_a)
    th = jnp.tanh(log_a)
    bv = jnp.sqrt(-2.0 * th / (1.0 - th)) * (gate_i * xc)
    r8 = lax.broadcasted_iota(I32, (C, WL), 0) & 7
    for d in (1, 2, 4):
        keep = r8 >= d
        a_sh = jnp.where(keep, pltpu.roll(a, d, 0), 1.0)
        b_sh = jnp.where(keep, pltpu.roll(bv, d, 0), 0.0)
        bv = a * b_sh + bv
        a = a * a_sh
    carry = hc_ref[...]
    hs = []
    for grp in range(C // 8):
        hg = bv[grp * 8:(grp + 1) * 8] + a[grp * 8:(grp + 1) * 8] * carry
        carry = hg[7:8]
        hs.append(hg)
    hc_ref[...] = carry
    hseq = jnp.concatenate(hs, axis=0)
    y = jax.nn.gelu(z_ref[0, :, 2048:2048 + WL]) * hseq
    o_ref[0, :, 1024:1024 + WL] = y.astype(o_ref.dtype)


def _cd_core(z, sgu_w, sgu_bt, conv_w, conv_b, wa, ba, wx, bx, lam):
    B, S, W = z.shape
    WL = LRU_HEADS * LRU_BLOCK
    full2 = lambda r, c: pl.BlockSpec((r, c), lambda b, n: (0, 0))
    full3 = lambda a, r, c: pl.BlockSpec((a, r, c), lambda b, n: (0, 0, 0))
    return pl.pallas_call(
        _cd_body,
        grid=(B, S // CHUNK),
        in_specs=[pl.BlockSpec((1, CHUNK, W), lambda b, n: (b, n, 0)),
                  full3(SGU_GROUPS, CHUNK, CHUNK), full2(CHUNK, SGU_GROUPS),
                  full2(CONV_WIDTH, WL), full2(1, WL),
                  full3(LRU_HEADS, LRU_BLOCK, LRU_BLOCK), full2(1, WL),
                  full3(LRU_HEADS, LRU_BLOCK, LRU_BLOCK), full2(1, WL), full2(1, WL)],
        out_specs=pl.BlockSpec((1, CHUNK, 1024 + WL), lambda b, n: (b, n, 0)),
        out_shape=jax.ShapeDtypeStruct((B, S, 1024 + WL), BF16),
        scratch_shapes=[pltpu.VMEM((8 + CHUNK, WL), F32), pltpu.VMEM((1, WL), F32)],
        compiler_params=_cparams("parallel", "arbitrary"),
        name="cd_core",
    )(z, sgu_w, sgu_bt, conv_w, conv_b, wa, ba, wx, bx, lam)


def _outproj_body(y_ref, w_ref, b_ref, x_ref, g_ref, beta_ref, o_ref):
    h = _dot(y_ref[...], w_ref[...]) + b_ref[...]
    o_ref[...] = _standardize(DEEPNORM_ALPHA * x_ref[...] + h) * g_ref[...] + beta_ref[...]


def _outproj_ln(y2d, w_bf16, b_row, x2d, g_row, beta_row):
    T, Ky = y2d.shape
    D = x2d.shape[1]
    row = pl.BlockSpec((1, D), lambda i: (0, 0))
    return pl.pallas_call(
        _outproj_body,
        grid=(T // ROW_TILE,),
        in_specs=[pl.BlockSpec((ROW_TILE, Ky), lambda i: (i, 0)),
                  pl.BlockSpec((Ky, D), lambda i: (0, 0)), row,
                  pl.BlockSpec((ROW_TILE, D), lambda i: (i, 0)), row, row],
        out_specs=pl.BlockSpec((ROW_TILE, D), lambda i: (i, 0)),
        out_shape=jax.ShapeDtypeStruct((T, D), F32),
        compiler_params=_cparams("parallel"),
        name="outproj_ln",
    )(y2d, w_bf16, b_row, x2d, g_row, beta_row)


def _router_body(x_ref, wrt_ref, bcol_ref, idx_ref, gate_ref, pos_ref, cnt_ref, carry_ref):
    i = pl.program_id(0)
    E, tm = carry_ref.shape

    @pl.when(i == 0)
    def _():
        carry_ref[...] = jnp.zeros_like(carry_ref)

    logits = lax.dot_general(wrt_ref[...], x_ref[...], (((1,), (1,)), ((), ())),
                             precision=lax.Precision.HIGHEST, preferred_element_type=F32)
    scores = jax.nn.sigmoid(logits)
    sel = scores + bcol_ref[...]
    eio = lax.broadcasted_iota(I32, (E, tm), 0)
    onehots, idxs, tops = [], [], []
    for _ in range(TOP_K):
        m = jnp.max(sel, axis=0, keepdims=True)
        idx = jnp.min(jnp.where(sel == m, eio, E), axis=0, keepdims=True)
        hit = eio == idx
        tops.append(jnp.sum(jnp.where(hit, scores, 0.0), axis=0, keepdims=True))
        sel = jnp.where(hit, -jnp.inf, sel)
        onehots.append(hit)
        idxs.append(idx)
    chosen = onehots[0]
    for hit in onehots[1:]:
        chosen = chosen | hit
    mask = jnp.where(chosen, 1.0, 0.0).astype(BF16)
    ti = lax.broadcasted_iota(I32, (tm, tm), 0)
    tj = lax.broadcasted_iota(I32, (tm, tm), 1)
    earlier = jnp.where(ti < tj, 1.0, 0.0).astype(BF16)
    before = _dot(mask, earlier) + carry_ref[...]
    total = _dot(mask, jnp.ones((tm, tm), BF16))
    carry_ref[...] = carry_ref[...] + total
    cnt_ref[...] = carry_ref[:, 0:LANES]
    tsum = tops[0]
    for t in tops[1:]:
        tsum = tsum + t
    inv = ROUTE_SCALE / tsum
    gate_ref[...] = jnp.concatenate([t * inv for t in tops], axis=0)
    idx_ref[...] = jnp.concatenate(idxs, axis=0)
    pos_ref[...] = jnp.concatenate(
        [jnp.sum(jnp.where(hit, before, 0.0), axis=0, keepdims=True) for hit in onehots], axis=0).astype(I32)


def _router(x2d, wrt, bcol):
    T, D = x2d.shape
    E = wrt.shape[0]
    tm = ROUTER_TILE
    out8 = lambda dt: jax.ShapeDtypeStruct((TOP_K, T), dt)
    spec8 = pl.BlockSpec((TOP_K, tm), lambda i: (0, i))
    return pl.pallas_call(
        _router_body,
        grid=(T // tm,),
        in_specs=[pl.BlockSpec((tm, D), lambda i: (i, 0)),
                  pl.BlockSpec((E, D), lambda i: (0, 0)),
                  pl.BlockSpec((E, 1), lambda i: (0, 0))],
        out_specs=[spec8, spec8, spec8, pl.BlockSpec((E, LANES), lambda i: (0, 0))],
        out_shape=[out8(I32), out8(F32), out8(I32), jax.ShapeDtypeStruct((E, LANES), F32)],
        scratch_shapes=[pltpu.VMEM((E, tm), F32)],
        compiler_params=_cparams("arbitrary"),
        name="router",
    )(x2d, wrt, bcol)


def _row_gather_start(idx_of, n_rows, src_hbm, dst, sem):
    def body(r, carry):
        pltpu.make_async_copy(src_hbm.at[pl.ds(idx_of(r), 1), :], dst.at[pl.ds(r, 1), :], sem).start()
        return carry
    lax.fori_loop(0, n_rows, body, 0, unroll=8)


def _row_gather_wait(src_hbm, dst, sem):
    pltpu.make_async_copy(src_hbm.at[pl.ds(0, dst.shape[0]), :], dst, sem).wait()


def _experts_body(be_ref, nr_ref, rt_cur_ref, rt_nxt_ref, x_hbm, wg_ref, wu_ref, wd_ref, y_ref,
                  xbuf_ref, wgb_ref, wub_ref, wdb_ref, sem):
    i = pl.program_id(0)
    nreal = nr_ref[0]
    slot = i % 2

    @pl.when(i == 0)
    def _():
        _row_gather_start(lambda r: rt_cur_ref[0, 0, r], EXPERT_ROWS, x_hbm, xbuf_ref.at[0], sem.at[0])

    @pl.when(i + 1 < nreal)
    def _():
        _row_gather_start(lambda r: rt_nxt_ref[0, 0, r], EXPERT_ROWS, x_hbm, xbuf_ref.at[1 - slot],
                          sem.at[1 - slot])

    @pl.when(i < nreal)
    def _():
        @pl.when((i == 0) | (be_ref[i] != be_ref[jnp.maximum(i - 1, 0)]))
        def _():
            wgb_ref[...] = wg_ref[0].astype(BF16)
            wub_ref[...] = wu_ref[0].astype(BF16)
            wdb_ref[...] = wd_ref[0].astype(BF16)

        _row_gather_wait(x_hbm, xbuf_ref.at[slot], sem.at[slot])
        xb = xbuf_ref[slot].astype(BF16)
        h = jax.nn.silu(_dot(xb, wgb_ref[...])) * _dot(xb, wub_ref[...])
        y_ref[...] = _dot(h.astype(BF16), wdb_ref[...])

    @pl.when(i >= nreal)
    def _():
        y_ref[...] = jnp.zeros_like(y_ref)


def _experts(x2d, row_tok, block_e, nreal, w_gate, w_up, w_down):
    T, D = x2d.shape
    E, _, H = w_gate.shape
    NB = block_e.shape[0]
    BM = EXPERT_ROWS
    rt3 = row_tok.reshape(NB, 1, BM)
    grid_spec = pltpu.PrefetchScalarGridSpec(
        num_scalar_prefetch=2,
        grid=(NB,),
        in_specs=[pl.BlockSpec((1, 1, BM), lambda i, be, nr: (i, 0, 0), memory_space=pltpu.SMEM),
                  pl.BlockSpec((1, 1, BM), lambda i, be, nr: (jnp.minimum(i + 1, NB - 1), 0, 0),
                               memory_space=pltpu.SMEM),
                  pl.BlockSpec(memory_space=pl.ANY),
                  pl.BlockSpec((1, D, H), lambda i, be, nr: (be[i], 0, 0)),
                  pl.BlockSpec((1, D, H), lambda i, be, nr: (be[i], 0, 0)),
                  pl.BlockSpec((1, H, D), lambda i, be, nr: (be[i], 0, 0))],
        out_specs=pl.BlockSpec((BM, D), lambda i, be, nr: (i, 0)),
        scratch_shapes=[pltpu.VMEM((2, BM, D), F32), pltpu.VMEM((D, H), BF16), pltpu.VMEM((D, H), BF16),
                        pltpu.VMEM((H, D), BF16), pltpu.SemaphoreType.DMA((2,))],
    )
    return pl.pallas_call(
        _experts_body,
        grid_spec=grid_spec,
        out_shape=jax.ShapeDtypeStruct((NB * BM, D), F32),
        compiler_params=_cparams("arbitrary"),
        name="experts",
    )(block_e, nreal, rt3, rt3, x2d, w_gate, w_up, w_down)


def _combine_body(d_cur_ref, d_nxt_ref, g_ref, x_ref, y_hbm, wsg_ref, wsu_ref, wsd_ref, lng_ref, lnb_ref,
                  o_ref, buf_ref, sem):
    i = pl.program_id(0)
    nt = pl.num_programs(0)
    slot = i % 2
    tm = COMBINE_TILE

    def start(d_ref, s):
        for k in range(TOP_K):
            _row_gather_start(lambda t: d_ref[0, k, t], tm, y_hbm, buf_ref.at[s, pl.ds(k * tm, tm), :], sem.at[s])

    @pl.when(i == 0)
    def _():
        start(d_cur_ref, 0)

    @pl.when(i + 1 < nt)
    def _():
        start(d_nxt_ref, 1 - slot)

    x = x_ref[...]
    xb = x.astype(BF16)
    hs = jax.nn.silu(_dot(xb, wsg_ref[...])) * _dot(xb, wsu_ref[...])
    acc = _dot(hs.astype(BF16), wsd_ref[...])
    _row_gather_wait(y_hbm, buf_ref.at[slot], sem.at[slot])
    g = g_ref[...]
    routed = g[:, 0:1] * buf_ref[slot, 0:tm, :]
    for k in range(1, TOP_K):
        routed = routed + g[:, k:k + 1] * buf_ref[slot, k * tm:(k + 1) * tm, :]
    o_ref[...] = _standardize(DEEPNORM_ALPHA * x + (routed + acc)) * lng_ref[...] + lnb_ref[...]


def _combine_ln(x2d, y_sorted, dest_tiles, gates, wsg, wsu, wsd, g_row, beta_row):
    T, D = x2d.shape
    H = wsg.shape[1]
    tm = COMBINE_TILE
    nt = T // tm
    row = pl.BlockSpec((1, D), lambda i: (0, 0))
    return pl.pallas_call(
        _combine_body,
        grid=(nt,),
        in_specs=[pl.BlockSpec((1, TOP_K, tm), lambda i: (i, 0, 0), memory_space=pltpu.SMEM),
                  pl.BlockSpec((1, TOP_K, tm), lambda i: (jnp.minimum(i + 1, nt - 1), 0, 0),
                               memory_space=pltpu.SMEM),
                  pl.BlockSpec((tm, TOP_K), lambda i: (i, 0)),
                  pl.BlockSpec((tm, D), lambda i: (i, 0)),
                  pl.BlockSpec(memory_space=pl.ANY),
                  pl.BlockSpec((D, H), lambda i: (0, 0)),
                  pl.BlockSpec((D, H), lambda i: (0, 0)),
                  pl.BlockSpec((H, D), lambda i: (0, 0)), row, row],
        out_specs=pl.BlockSpec((tm, D), lambda i: (i, 0)),
        out_shape=jax.ShapeDtypeStruct((T, D), F32),
        scratch_shapes=[pltpu.VMEM((2, TOP_K * tm, D), F32), pltpu.SemaphoreType.DMA((2,))],
        compiler_params=_cparams("arbitrary"),
        name="combine_ln",
    )(dest_tiles, dest_tiles, gates, x2d, y_sorted, wsg, wsu, wsd, g_row, beta_row)


def _moe_ln(x2d, w_router, b_router, w_gate, w_up, w_down, ws_gate, ws_up, ws_down, g_row, beta_row):
    T, D = x2d.shape
    E = w_router.shape[1]
    BM = EXPERT_ROWS
    idx_t, gate_t, pos_t, cnt = _router(x2d, jnp.transpose(w_router), b_router.reshape(E, 1))
    counts = cnt[:, 0].astype(I32)
    padded = (counts + BM - 1) // BM * BM
    pend = jnp.cumsum(padded)
    pstart = pend - padded
    NB = (T * TOP_K) // BM + E
    nreal = (pend[-1] // BM).astype(I32).reshape(1)
    block_e = jnp.minimum(jnp.searchsorted(pend, jnp.arange(NB, dtype=I32) * BM, side='right'), E - 1).astype(I32)
    dest_t = jnp.take(pstart, idx_t) + pos_t
    tok = jnp.broadcast_to(jnp.arange(T, dtype=I32)[None, :], (TOP_K, T))
    row_tok = jnp.zeros((NB * BM,), I32).at[dest_t.reshape(-1)].set(tok.reshape(-1), unique_indices=True)
    y_sorted = _experts(x2d, row_tok, block_e, nreal, w_gate, w_up, w_down)
    nt = T // COMBINE_TILE
    dest_tiles = jnp.transpose(dest_t.reshape(TOP_K, nt, COMBINE_TILE), (1, 0, 2))
    return _combine_ln(x2d, y_sorted, dest_tiles, jnp.transpose(gate_t), ws_gate.astype(BF16),
                       ws_up.astype(BF16), ws_down.astype(BF16), g_row, beta_row)


def _position_tables(S):
    pos = jnp.arange(S, dtype=F32)[:, None]
    attn_freq = ROPE_THETA ** (-jnp.arange(0, A_HEAD_DIM, 2, dtype=F32) / A_HEAD_DIM)
    ang = pos * attn_freq[None, :]
    cosa = jnp.tile(jnp.cos(ang), (1, 4))
    sina = jnp.tile(jnp.concatenate([-jnp.sin(ang), jnp.sin(ang)], axis=1), (1, 2))
    ret_freq = ROPE_THETA ** (-jnp.linspace(0.0, 1.0, R_KEY_DIM // 2, dtype=F32))
    angr = pos * ret_freq[None, :]
    cosr = jnp.tile(jnp.cos(angr), (1, 2))
    sinr = jnp.concatenate([-jnp.sin(angr), jnp.sin(angr)], axis=1)
    C = CHUNK
    log_g = jnp.log1p(-jnp.exp2(-5.0 - jnp.arange(R_HEADS, dtype=F32)))
    idx = jnp.arange(C, dtype=F32)
    diff = idx[:, None] - idx[None, :]
    decay = jnp.where(diff >= 0, jnp.exp(log_g[:, None, None] * jnp.maximum(diff, 0.0)), 0.0)
    kdec = jnp.exp(log_g[:, None] * (C - 1.0 - idx)[None, :])
    qdec = jnp.exp(log_g[:, None] * (idx + 1.0)[None, :])
    bc = lambda t: jnp.broadcast_to(t[:, :, None], (R_HEADS, C, R_KEY_DIM))
    gch = jnp.exp(log_g * C)
    return cosa, sina, cosr, sinr, decay, bc(qdec), bc(kdec), gch


def kernel(x, ab_w_in, ab_b_in, ab_sinks, ab_w_out, ab_b_out, cd_w_in, cd_b_in, sgu_w, sgu_b, conv_w, conv_b,
           lru_wa, lru_ba, lru_wx, lru_bx, lru_lambda, cd_w_out, cd_b_out, ln_mix_g, ln_mix_b, router_w, router_b,
           exp_w_gate, exp_w_up, exp_w_down, shared_w_gate, shared_w_up, shared_w_down, ln_ffn_g, ln_ffn_b):
    B, S, D = x.shape
    T = B * S
    tables = _position_tables(S)
    row = lambda v: v.reshape(1, -1)
    x2d = x.reshape(T, D)
    for layer in range(DEPTH):
        j = layer // 2
        if layer % 2 == 0:
            z = _inproj(x2d, ab_w_in[j].astype(BF16), row(ab_b_in[j]))
            y = _ab_core(z.reshape(B, S, -1), ab_sinks[j], tables)
            w_out, b_out = ab_w_out[j], ab_b_out[j]
        else:
            z = _inproj(x2d, cd_w_in[j].astype(BF16), row(cd_b_in[j]))
            y = _cd_core(z.reshape(B, S, -1), sgu_w[j], jnp.transpose(sgu_b[j]), conv_w[j], row(conv_b[j]),
                         lru_wa[j].astype(BF16), row(lru_ba[j]), lru_wx[j].astype(BF16), row(lru_bx[j]),
                         row(lru_lambda[j]))
            w_out, b_out = cd_w_out[j], cd_b_out[j]
        x2d = _outproj_ln(y.reshape(T, -1), w_out.astype(BF16), row(b_out), x2d,
                          row(ln_mix_g[layer]), row(ln_mix_b[layer]))
        x2d = _moe_ln(x2d, router_w[layer], router_b[layer], exp_w_gate[layer], exp_w_up[layer],
                      exp_w_down[layer], shared_w_gate[layer], shared_w_up[layer], shared_w_down[layer],
                      row(ln_ffn_g[layer]), row(ln_ffn_b[layer]))
    return x2d.reshape(B, S, D)
```

```python
import functools
import math

import jax
import jax.numpy as jnp
from jax import lax
from jax.experimental import pallas as pl
from jax.experimental.pallas import tpu as pltpu

F32 = jnp.float32
BF16 = jnp.bfloat16
I32 = jnp.int32

CHUNK = 128
A_HEADS, A_KV_HEADS, A_HEAD_DIM = 16, 2, 64
R_HEADS, R_KEY_DIM, R_VAL_DIM = 4, 128, 256
SGU_GROUPS, SGU_GROUP_DIM = 4, 256
LRU_HEADS, LRU_BLOCK = 5, 256
CONV_WIDTH = 4
LRU_C = 8.0
TOP_K = 8
ROUTE_SCALE = 2.5
ROPE_THETA = 10000.0
LN_EPS = 1e-5
NEG_INF = -1e30
DEPTH = 4
DEEPNORM_ALPHA = (2 * DEPTH) ** 0.25

LANES = 128
VMEM_LIMIT_BYTES = 48 * 1024 * 1024
ROW_TILE = 256
ROUTER_TILE = 512
EXPERT_ROWS = 256
COMBINE_TILE = 128


def _cparams(*sem):
    return pltpu.CompilerParams(dimension_semantics=sem, vmem_limit_bytes=VMEM_LIMIT_BYTES)


def _standardize(x):
    xc = x - jnp.mean(x, axis=-1, keepdims=True)
    return xc * lax.rsqrt(jnp.mean(xc * xc, axis=-1, keepdims=True) + LN_EPS)


def _dot(a, b):
    return jnp.dot(a, b, preferred_element_type=F32)


def _dot_nt(a, b):
    return lax.dot_general(a, b, (((1,), (1,)), ((), ())), preferred_element_type=F32)


def _inproj_body(x_ref, w_ref, b_ref, o_ref):
    o_ref[...] = _dot(x_ref[...].astype(BF16), w_ref[...]) + b_ref[...]


def _inproj(x2d, w_bf16, b_row):
    T, D = x2d.shape
    N = w_bf16.shape[1]
    return pl.pallas_call(
        _inproj_body,
        grid=(T // ROW_TILE,),
        in_specs=[pl.BlockSpec((ROW_TILE, D), lambda i: (i, 0)),
                  pl.BlockSpec((D, N), lambda i: (0, 0)),
                  pl.BlockSpec((1, N), lambda i: (0, 0))],
        out_specs=pl.BlockSpec((ROW_TILE, N), lambda i: (i, 0)),
        out_shape=jax.ShapeDtypeStruct((T, N), F32),
        compiler_params=_cparams("parallel"),
        name="inproj",
    )(x2d, w_bf16, b_row)


def _ab_body(sinks_ref, gch_ref, z_ref, cosa_ref, sina_ref, cosr_ref, sinr_ref, decay_ref, qdec_ref, kdec_ref,
             o_ref, kprev_ref, vprev_ref, state_ref):
    n = pl.program_id(1)
    C = CHUNK

    @pl.when(n == 0)
    def _():
        kprev_ref[...] = jnp.zeros_like(kprev_ref)
        vprev_ref[...] = jnp.zeros_like(vprev_ref)
        state_ref[...] = jnp.zeros_like(state_ref)

    lane = lax.broadcasted_iota(I32, (C, LANES), 1)
    lo_half = lane < 64
    first_32 = (lane & 32) == 0
    cosa = cosa_ref[...]
    sina = sina_ref[...]

    def rope64(x):
        swapped = jnp.where(first_32, pltpu.roll(x, 96, 1), pltpu.roll(x, 32, 1))
        return x * cosa + swapped * sina

    k = rope64(z_ref[0, :, 1024:1152])
    v = z_ref[0, :, 1152:1280]
    kw = jnp.concatenate([kprev_ref[...], k], axis=0)
    vw = jnp.concatenate([vprev_ref[...], v], axis=0)
    kprev_ref[...] = k
    vprev_ref[...] = v
    kw_sw = pltpu.roll(kw, 64, 1)
    vw_sw = pltpu.roll(vw, 64, 1)
    lane2 = lax.broadcasted_iota(I32, (2 * C, LANES), 1)
    lo2 = lane2 < 64

    rows = 8 * C
    qi = lax.broadcasted_iota(I32, (rows, 2 * C), 0) & (C - 1)
    kj = lax.broadcasted_iota(I32, (rows, 2 * C), 1)
    valid = (kj > qi) & (kj <= qi + C) & ((n > 0) | (kj >= C))
    rblk = lax.broadcasted_iota(I32, (rows, 1), 0) // C

    for g in range(A_KV_HEADS):
        if g == 0:
            kdup = jnp.where(lo2, kw, kw_sw)
            vlo = jnp.where(lo2, vw, 0.0)
            vhi = jnp.where(lo2, 0.0, vw_sw)
        else:
            kdup = jnp.where(lo2, kw_sw, kw)
            vlo = jnp.where(lo2, vw_sw, 0.0)
            vhi = jnp.where(lo2, 0.0, vw)
        qs = [rope64(z_ref[0, :, c * LANES:(c + 1) * LANES]) * (A_HEAD_DIM ** -0.5)
              for c in range(4 * g, 4 * g + 4)]
        qg = jnp.concatenate([jnp.where(lo_half, qc, 0.0) for qc in qs]
                             + [jnp.where(lo_half, 0.0, qc) for qc in qs], axis=0).astype(BF16)
        s = _dot_nt(qg, kdup.astype(BF16))
        s = jnp.where(valid, s, NEG_INF)
        sink = jnp.zeros((rows, 1), F32)
        for j in range(8):
            head = 8 * g + (2 * j if j < 4 else 2 * (j - 4) + 1)
            sink = jnp.where(rblk == j, sinks_ref[head], sink)
        m = jnp.maximum(jnp.max(s, axis=-1, keepdims=True), sink)
        e = jnp.exp(s - m)
        den = jnp.sum(e, axis=-1, keepdims=True) + jnp.exp(sink - m)
        p = (e * (1.0 / den)).astype(BF16)
        p2 = jnp.concatenate([p[:4 * C], p[4 * C:]], axis=1)
        vblk = jnp.concatenate([vlo, vhi], axis=0).astype(BF16)
        o = _dot(p2, vblk)
        for c in range(4):
            col = (4 * g + c) * LANES
            o_ref[0, :, col:col + LANES] = o[c * C:(c + 1) * C].astype(o_ref.dtype)

    cosr = cosr_ref[...]
    sinr = sinr_ref[...]

    def rope128(x):
        return x * cosr + pltpu.roll(x, 64, 1) * sinr

    q0, k0, v0, g0 = 1280, 1792, 2304, 3328
    for h in range(R_HEADS):
        qh = rope128(z_ref[0, :, q0 + h * 128:q0 + (h + 1) * 128])
        kh = rope128(z_ref[0, :, k0 + h * 128:k0 + (h + 1) * 128]) * (R_KEY_DIM ** -0.5)
        vb = z_ref[0, :, v0 + h * 256:v0 + (h + 1) * 256].astype(BF16)
        gh = z_ref[0, :, g0 + h * 256:g0 + (h + 1) * 256]
        s = _dot_nt(qh.astype(BF16), kh.astype(BF16)) * decay_ref[h]
        st = state_ref[h]
        o = _dot(s.astype(BF16), vb) + _dot((qh * qdec_ref[h]).astype(BF16), st.astype(BF16))
        kd_t = jnp.transpose(kh * kdec_ref[h]).astype(BF16)
        state_ref[h] = gch_ref[h] * st + _dot(kd_t, vb)
        y = jax.nn.silu(gh) * _standardize(o)
        col = 1024 + h * 256
        o_ref[0, :, col:col + 256] = y.astype(o_ref.dtype)


def _ab_core(z, sinks, tables):
    B, S, W = z.shape
    cosa, sina, cosr, sinr, decay, qdec, kdec, gch = tables
    smem = pl.BlockSpec(memory_space=pltpu.SMEM)
    tab = pl.BlockSpec((CHUNK, LANES), lambda b, n: (n, 0))
    const3 = pl.BlockSpec((R_HEADS, CHUNK, CHUNK), lambda b, n: (0, 0, 0))
    return pl.pallas_call(
        _ab_body,
        grid=(B, S // CHUNK),
        in_specs=[smem, smem,
                  pl.BlockSpec((1, CHUNK, W), lambda b, n: (b, n, 0)),
                  tab, tab, tab, tab, const3, const3, const3],
        out_specs=pl.BlockSpec((1, CHUNK, 2048), lambda b, n: (b, n, 0)),
        out_shape=jax.ShapeDtypeStruct((B, S, 2048), BF16),
        scratch_shapes=[pltpu.VMEM((CHUNK, LANES), F32), pltpu.VMEM((CHUNK, LANES), F32),
                        pltpu.VMEM((R_HEADS, R_KEY_DIM, R_VAL_DIM), F32)],
        compiler_params=_cparams("parallel", "arbitrary"),
        name="ab_core",
    )(sinks, gch, z, cosa, sina, cosr, sinr, decay, qdec, kdec)


def _cd_body(z_ref, sw_ref, sbt_ref, cw_ref, cb_ref, wa_ref, ba_ref, wx_ref, bx_ref, lam_ref,
             o_ref, xwin_ref, hc_ref):
    n = pl.program_id(1)
    C = CHUNK
    WL = LRU_HEADS * LRU_BLOCK

    @pl.when(n == 0)
    def _():
        xwin_ref[0:8, :] = jnp.zeros((8, WL), F32)
        hc_ref[...] = jnp.zeros_like(hc_ref)

    gu = jax.nn.gelu(z_ref[0, :, 0:1024])
    vn = _standardize(jax.nn.gelu(z_ref[0, :, 1024:2048])).astype(BF16)
    ri = lax.broadcasted_iota(I32, (C, C), 0)
    ci = lax.broadcasted_iota(I32, (C, C), 1)
    for g in range(SGU_GROUPS):
        w = jnp.where(ci <= ri, sw_ref[g], 0.0).astype(BF16)
        col = g * SGU_GROUP_DIM
        mixed = _dot(w, vn[:, col:col + SGU_GROUP_DIM]) + sbt_ref[:, g:g + 1]
        o_ref[0, :, col:col + SGU_GROUP_DIM] = (gu[:, col:col + SGU_GROUP_DIM] * mixed).astype(o_ref.dtype)

    xr = z_ref[0, :, 3328:3328 + WL]
    xwin_ref[8:8 + C, :] = xr
    xc = (cw_ref[3:4, :] * xr + cw_ref[2:3, :] * xwin_ref[7:7 + C, :] + cw_ref[1:2, :] * xwin_ref[6:6 + C, :]
          + cw_ref[0:1, :] * xwin_ref[5:5 + C, :] + cb_ref[...])
    xwin_ref[0:8, :] = xr[C - 8:C]
    xcb = xc.astype(BF16)
    ga, gi = [], []
    for h in range(LRU_HEADS):
        xb = xcb[:, h * LRU_BLOCK:(h + 1) * LRU_BLOCK]
        ga.append(_dot(xb, wa_ref[h]))
        gi.append(_dot(xb, wx_ref[h]))
    gate_r = jax.nn.sigmoid(jnp.concatenate(ga, axis=1) + ba_ref[...])
    gate_i = jax.nn.sigmoid(jnp.concatenate(gi, axis=1) + bx_ref[...])
    nl = -lam_ref[...]
    softplus = jnp.maximum(nl, 0.0) + jnp.log1p(jnp.exp(-jnp.abs(nl)))
    log_a = -LRU_C * gate_r * softplus
    a = jnp.exp(log_a)
    th = jnp.tanh(log_a)
    bv = jnp.sqrt(-2.0 * th / (1.0 - th)) * (gate_i * xc)
    r8 = lax.broadcasted_iota(I32, (C, WL), 0) & 7
    for d in (1, 2, 4):
        keep = r8 >= d
        a_sh = jnp.where(keep, pltpu.roll(a, d, 0), 1.0)
        b_sh = jnp.where(keep, pltpu.roll(bv, d, 0), 0.0)
        bv = a * b_sh + bv
        a = a * a_sh
    carry = hc_ref[...]
    hs = []
    for grp in range(C // 8):
        hg = bv[grp * 8:(grp + 1) * 8] + a[grp * 8:(grp + 1) * 8] * carry
        carry = hg[7:8]
        hs.append(hg)
    hc_ref[...] = carry
    hseq = jnp.concatenate(hs, axis=0)
    y = jax.nn.gelu(z_ref[0, :, 2048:2048 + WL]) * hseq
    o_ref[0, :, 1024:1024 + WL] = y.astype(o_ref.dtype)


def _cd_core(z, sgu_w, sgu_bt, conv_w, conv_b, wa, ba, wx, bx, lam):
    B, S, W = z.shape
    WL = LRU_HEADS * LRU_BLOCK
    full2 = lambda r, c: pl.BlockSpec((r, c), lambda b, n: (0, 0))
    full3 = lambda a, r, c: pl.BlockSpec((a, r, c), lambda b, n: (0, 0, 0))
    return pl.pallas_call(
        _cd_body,
        grid=(B, S // CHUNK),
        in_specs=[pl.BlockSpec((1, CHUNK, W), lambda b, n: (b, n, 0)),
                  full3(SGU_GROUPS, CHUNK, CHUNK), full2(CHUNK, SGU_GROUPS),
                  full2(CONV_WIDTH, WL), full2(1, WL),
                  full3(LRU_HEADS, LRU_BLOCK, LRU_BLOCK), full2(1, WL),
                  full3(LRU_HEADS, LRU_BLOCK, LRU_BLOCK), full2(1, WL), full2(1, WL)],
        out_specs=pl.BlockSpec((1, CHUNK, 1024 + WL), lambda b, n: (b, n, 0)),
        out_shape=jax.ShapeDtypeStruct((B, S, 1024 + WL), BF16),
        scratch_shapes=[pltpu.VMEM((8 + CHUNK, WL), F32), pltpu.VMEM((1, WL), F32)],
        compiler_params=_cparams("parallel", "arbitrary"),
        name="cd_core",
    )(z, sgu_w, sgu_bt, conv_w, conv_b, wa, ba, wx, bx, lam)


def _to_row_tiles(ref, val):
    n = val.shape[0]
    for j in range(val.shape[1] // LANES):
        ref[pl.ds(j, n, stride=8), :] = val[:, j * LANES:(j + 1) * LANES]


def _from_row_tiles(ref, first_row, n):
    return jnp.concatenate([ref[pl.ds(first_row * 8 + j, n, stride=8), :] for j in range(8)], axis=1)


def _outproj_body(y_ref, w_ref, b_ref, x_ref, g_ref, beta_ref, o_ref, otile_ref):
    h = _dot(y_ref[...], w_ref[...]) + b_ref[...]
    xn = _standardize(DEEPNORM_ALPHA * x_ref[...] + h) * g_ref[...] + beta_ref[...]
    o_ref[...] = xn
    _to_row_tiles(otile_ref, xn)


def _outproj_ln(y2d, w_bf16, b_row, x2d, g_row, beta_row):
    T, Ky = y2d.shape
    D = x2d.shape[1]
    row = pl.BlockSpec((1, D), lambda i: (0, 0))
    return pl.pallas_call(
        _outproj_body,
        grid=(T // ROW_TILE,),
        in_specs=[pl.BlockSpec((ROW_TILE, Ky), lambda i: (i, 0)),
                  pl.BlockSpec((Ky, D), lambda i: (0, 0)), row,
                  pl.BlockSpec((ROW_TILE, D), lambda i: (i, 0)), row, row],
        out_specs=[pl.BlockSpec((ROW_TILE, D), lambda i: (i, 0)),
                   pl.BlockSpec((ROW_TILE * 8, LANES), lambda i: (i, 0))],
        out_shape=[jax.ShapeDtypeStruct((T, D), F32), jax.ShapeDtypeStruct((T * 8, LANES), F32)],
        compiler_params=_cparams("parallel"),
        name="outproj_ln",
    )(y2d, w_bf16, b_row, x2d, g_row, beta_row)


def _router_body(x_ref, wrt_ref, bcol_ref, idx_ref, gate_ref, pos_ref, cnt_ref, carry_ref):
    i = pl.program_id(0)
    E, tm = carry_ref.shape

    @pl.when(i == 0)
    def _():
        carry_ref[...] = jnp.zeros_like(carry_ref)

    logits = lax.dot_general(wrt_ref[...], x_ref[...], (((1,), (1,)), ((), ())),
                             precision=lax.Precision.HIGHEST, preferred_element_type=F32)
    scores = jax.nn.sigmoid(logits)
    sel = scores + bcol_ref[...]
    eio = lax.broadcasted_iota(I32, (E, tm), 0)
    onehots, idxs, tops = [], [], []
    for _ in range(TOP_K):
        m = jnp.max(sel, axis=0, keepdims=True)
        idx = jnp.min(jnp.where(sel == m, eio, E), axis=0, keepdims=True)
        hit = eio == idx
        tops.append(jnp.sum(jnp.where(hit, scores, 0.0), axis=0, keepdims=True))
        sel = jnp.where(hit, -jnp.inf, sel)
        onehots.append(hit)
        idxs.append(idx)
    chosen = onehots[0]
    for hit in onehots[1:]:
        chosen = chosen | hit
    mask = jnp.where(chosen, 1.0, 0.0).astype(BF16)
    ti = lax.broadcasted_iota(I32, (tm, tm), 0)
    tj = lax.broadcasted_iota(I32, (tm, tm), 1)
    earlier = jnp.where(ti < tj, 1.0, 0.0).astype(BF16)
    before = _dot(mask, earlier) + carry_ref[...]
    total = _dot(mask, jnp.ones((tm, tm), BF16))
    carry_ref[...] = carry_ref[...] + total
    cnt_ref[...] = carry_ref[:, 0:LANES]
    tsum = tops[0]
    for t in tops[1:]:
        tsum = tsum + t
    inv = ROUTE_SCALE / tsum
    gate_ref[...] = jnp.concatenate([t * inv for t in tops], axis=0)
    idx_ref[...] = jnp.concatenate(idxs, axis=0)
    pos_ref[...] = jnp.concatenate(
        [jnp.sum(jnp.where(hit, before, 0.0), axis=0, keepdims=True) for hit in onehots], axis=0).astype(I32)


def _router(x2d, wrt, bcol):
    T, D = x2d.shape
    E = wrt.shape[0]
    tm = ROUTER_TILE
    out8 = lambda dt: jax.ShapeDtypeStruct((TOP_K, T), dt)
    spec8 = pl.BlockSpec((TOP_K, tm), lambda i: (0, i))
    return pl.pallas_call(
        _router_body,
        grid=(T // tm,),
        in_specs=[pl.BlockSpec((tm, D), lambda i: (i, 0)),
                  pl.BlockSpec((E, D), lambda i: (0, 0)),
                  pl.BlockSpec((E, 1), lambda i: (0, 0))],
        out_specs=[spec8, spec8, spec8, pl.BlockSpec((E, LANES), lambda i: (0, 0))],
        out_shape=[out8(I32), out8(F32), out8(I32), jax.ShapeDtypeStruct((E, LANES), F32)],
        scratch_shapes=[pltpu.VMEM((E, tm), F32)],
        compiler_params=_cparams("arbitrary"),
        name="router",
    )(x2d, wrt, bcol)


def _tile_gather_start(idx_of, n_rows, src_hbm, dst, sem):
    def body(p, carry):
        for u in range(2):
            r = 2 * p + u
            pltpu.make_async_copy(src_hbm.at[pl.ds(pl.multiple_of(idx_of(r) * 8, 8), 8), :],
                                  dst.at[pl.ds(pl.multiple_of(r * 8, 8), 8), :], sem).start(priority=u)
        return carry
    lax.fori_loop(0, n_rows // 2, body, 0, unroll=4)


def _tile_gather_wait(src_hbm, dst, sem):
    pltpu.make_async_copy(src_hbm.at[pl.ds(0, dst.shape[0]), :], dst, sem).wait()


def _experts_body(be_ref, nr_ref, rt_cur_ref, rt_nxt_ref, x_hbm, wg_ref, wu_ref, wd_ref, y_ref,
                  xbuf_ref, wgb_ref, wub_ref, wdb_ref, sem):
    i = pl.program_id(0)
    nb = pl.num_programs(0)
    nreal = nr_ref[0]
    slot = i % 2
    BM = EXPERT_ROWS

    @pl.when(i == 0)
    def _():
        _tile_gather_start(lambda r: rt_cur_ref[0, 0, r], BM, x_hbm, xbuf_ref.at[0], sem.at[0])

    @pl.when((i < nreal) & (i + 1 < nb))
    def _():
        _tile_gather_start(lambda r: rt_nxt_ref[0, 0, r], BM, x_hbm, xbuf_ref.at[1 - slot], sem.at[1 - slot])

    @pl.when(i <= nreal)
    def _():
        _tile_gather_wait(x_hbm, xbuf_ref.at[slot], sem.at[slot])

    @pl.when(i < nreal)
    def _():
        @pl.when((i == 0) | (be_ref[i] != be_ref[jnp.maximum(i - 1, 0)]))
        def _():
            wgb_ref[...] = wg_ref[0].astype(BF16)
            wub_ref[...] = wu_ref[0].astype(BF16)
            wdb_ref[...] = wd_ref[0].astype(BF16)

        xb = _from_row_tiles(xbuf_ref.at[slot], 0, BM).astype(BF16)
        h = jax.nn.silu(_dot(xb, wgb_ref[...])) * _dot(xb, wub_ref[...])
        _to_row_tiles(y_ref, _dot(h.astype(BF16), wdb_ref[...]))

    @pl.when(i >= nreal)
    def _():
        y_ref[...] = jnp.zeros_like(y_ref)


def _experts(x_tiles, row_tok, block_e, nreal, w_gate, w_up, w_down, layer):
    _, E, D, H = w_gate.shape
    NB = block_e.shape[0]
    BM = EXPERT_ROWS
    rt3 = row_tok.reshape(NB, 1, BM)
    grid_spec = pltpu.PrefetchScalarGridSpec(
        num_scalar_prefetch=2,
        grid=(NB,),
        in_specs=[pl.BlockSpec((1, 1, BM), lambda i, be, nr: (i, 0, 0), memory_space=pltpu.SMEM),
                  pl.BlockSpec((1, 1, BM), lambda i, be, nr: (jnp.minimum(i + 1, NB - 1), 0, 0),
                               memory_space=pltpu.SMEM),
                  pl.BlockSpec(memory_space=pl.ANY),
                  pl.BlockSpec((None, 1, D, H), lambda i, be, nr: (layer, be[i], 0, 0)),
                  pl.BlockSpec((None, 1, D, H), lambda i, be, nr: (layer, be[i], 0, 0)),
                  pl.BlockSpec((None, 1, H, D), lambda i, be, nr: (layer, be[i], 0, 0))],
        out_specs=pl.BlockSpec((BM * 8, LANES), lambda i, be, nr: (i, 0)),
        scratch_shapes=[pltpu.VMEM((2, BM * 8, LANES), F32), pltpu.VMEM((D, H), BF16), pltpu.VMEM((D, H), BF16),
                        pltpu.VMEM((H, D), BF16), pltpu.SemaphoreType.DMA((2,))],
    )
    return pl.pallas_call(
        _experts_body,
        grid_spec=grid_spec,
        out_shape=jax.ShapeDtypeStruct((NB * BM * 8, LANES), F32),
        compiler_params=_cparams("arbitrary"),
        name="experts",
    )(block_e, nreal, rt3, rt3, x_tiles, w_gate, w_up, w_down)


def _combine_body(d_cur_ref, d_nxt_ref, g_ref, x_ref, y_hbm, wsg_ref, wsu_ref, wsd_ref, lng_ref, lnb_ref,
                  o_ref, buf_ref, sem):
    i = pl.program_id(0)
    nt = pl.num_programs(0)
    slot = i % 2
    tm = COMBINE_TILE

    def start(d_ref, s):
        _tile_gather_start(lambda r: d_ref[0, 0, r], TOP_K * tm, y_hbm, buf_ref.at[s], sem.at[s])

    @pl.when(i == 0)
    def _():
        start(d_cur_ref, 0)

    @pl.when(i + 1 < nt)
    def _():
        start(d_nxt_ref, 1 - slot)

    x = x_ref[...]
    xb = x.astype(BF16)
    hs = jax.nn.silu(_dot(xb, wsg_ref[...])) * _dot(xb, wsu_ref[...])
    acc = _dot(hs.astype(BF16), wsd_ref[...])
    _tile_gather_wait(y_hbm, buf_ref.at[slot], sem.at[slot])
    g = g_ref[...]
    routed = g[:, 0:1] * _from_row_tiles(buf_ref.at[slot], 0, tm)
    for k in range(1, TOP_K):
        routed = routed + g[:, k:k + 1] * _from_row_tiles(buf_ref.at[slot], k * tm, tm)
    o_ref[...] = _standardize(DEEPNORM_ALPHA * x + (routed + acc)) * lng_ref[...] + lnb_ref[...]


def _combine_ln(x2d, y_tiles, dest_tiles, gates, wsg, wsu, wsd, g_row, beta_row):
    T, D = x2d.shape
    H = wsg.shape[1]
    tm = COMBINE_TILE
    nt = T // tm
    row = pl.BlockSpec((1, D), lambda i: (0, 0))
    return pl.pallas_call(
        _combine_body,
        grid=(nt,),
        in_specs=[pl.BlockSpec((1, 1, TOP_K * tm), lambda i: (i, 0, 0), memory_space=pltpu.SMEM),
                  pl.BlockSpec((1, 1, TOP_K * tm), lambda i: (jnp.minimum(i + 1, nt - 1), 0, 0),
                               memory_space=pltpu.SMEM),
                  pl.BlockSpec((tm, TOP_K), lambda i: (i, 0)),
                  pl.BlockSpec((tm, D), lambda i: (i, 0)),
                  pl.BlockSpec(memory_space=pl.ANY),
                  pl.BlockSpec((D, H), lambda i: (0, 0)),
                  pl.BlockSpec((D, H), lambda i: (0, 0)),
                  pl.BlockSpec((H, D), lambda i: (0, 0)), row, row],
        out_specs=pl.BlockSpec((tm, D), lambda i: (i, 0)),
        out_shape=jax.ShapeDtypeStruct((T, D), F32),
        scratch_shapes=[pltpu.VMEM((2, TOP_K * tm * 8, LANES), F32), pltpu.SemaphoreType.DMA((2,))],
        compiler_params=_cparams("arbitrary"),
        name="combine_ln",
    )(dest_tiles, dest_tiles, gates, x2d, y_tiles, wsg, wsu, wsd, g_row, beta_row)


def _moe_ln(x2d, x_tiles, w_router, b_router, w_gate, w_up, w_down, ws_gate, ws_up, ws_down, g_row, beta_row,
            layer):
    T, D = x2d.shape
    E = w_router.shape[1]
    BM = EXPERT_ROWS
    NB = (T * TOP_K) // BM + E
    idx_t, gate_t, pos_t, cnt = _router(x2d, jnp.transpose(w_router), b_router.reshape(E, 1))
    counts = cnt[:, 0].astype(I32)
    padded = (counts + BM - 1) // BM * BM
    pend = jnp.cumsum(padded)
    pstart = pend - padded
    nreal = (pend[-1] // BM).astype(I32).reshape(1)
    block_start = jnp.arange(NB, dtype=I32) * BM
    block_e = jnp.minimum(jnp.sum((pend[None, :] <= block_start[:, None]).astype(I32), axis=1), E - 1)
    eids = jnp.arange(E, dtype=I32)
    dest_t = jnp.sum(jnp.where(idx_t[:, :, None] == eids, pstart, 0), axis=-1) + pos_t
    span = 2 * T
    tok = jnp.arange(T, dtype=I32)
    real_keys = (idx_t * span + tok[None, :]).reshape(-1)
    j = jnp.arange(BM, dtype=I32)
    pad_keys = jnp.where(j[None, :] < (padded - counts)[:, None], eids[:, None] * span + T + j[None, :],
                         jnp.iinfo(jnp.int32).max).reshape(-1)
    keys = jnp.sort(jnp.concatenate([real_keys, pad_keys]))
    low = keys % span
    row_tok = jnp.where((keys < E * span) & (low < T), low, 0)
    y_tiles = _experts(x_tiles, row_tok, block_e, nreal, w_gate, w_up, w_down, layer)
    nt = T // COMBINE_TILE
    dest_tiles = jnp.transpose(dest_t.reshape(TOP_K, nt, COMBINE_TILE), (1, 0, 2)).reshape(nt, 1, -1)
    return _combine_ln(x2d, y_tiles, dest_tiles, jnp.transpose(gate_t), ws_gate.astype(BF16),
                       ws_up.astype(BF16), ws_down.astype(BF16), g_row, beta_row)


def _position_tables(S):
    pos = jnp.arange(S, dtype=F32)[:, None]
    attn_freq = ROPE_THETA ** (-jnp.arange(0, A_HEAD_DIM, 2, dtype=F32) / A_HEAD_DIM)
    ang = pos * attn_freq[None, :]
    cosa = jnp.tile(jnp.cos(ang), (1, 4))
    sina = jnp.tile(jnp.concatenate([-jnp.sin(ang), jnp.sin(ang)], axis=1), (1, 2))
    ret_freq = ROPE_THETA ** (-jnp.linspace(0.0, 1.0, R_KEY_DIM // 2, dtype=F32))
    angr = pos * ret_freq[None, :]
    cosr = jnp.tile(jnp.cos(angr), (1, 2))
    sinr = jnp.concatenate([-jnp.sin(angr), jnp.sin(angr)], axis=1)
    C = CHUNK
    log_g = jnp.log1p(-jnp.exp2(-5.0 - jnp.arange(R_HEADS, dtype=F32)))
    idx = jnp.arange(C, dtype=F32)
    diff = idx[:, None] - idx[None, :]
    decay = jnp.where(diff >= 0, jnp.exp(log_g[:, None, None] * jnp.maximum(diff, 0.0)), 0.0)
    kdec = jnp.exp(log_g[:, None] * (C - 1.0 - idx)[None, :])
    qdec = jnp.exp(log_g[:, None] * (idx + 1.0)[None, :])
    bc = lambda t: jnp.broadcast_to(t[:, :, None], (R_HEADS, C, R_KEY_DIM))
    gch = jnp.exp(log_g * C)
    return cosa, sina, cosr, sinr, decay, bc(qdec), bc(kdec), gch


def kernel(x, ab_w_in, ab_b_in, ab_sinks, ab_w_out, ab_b_out, cd_w_in, cd_b_in, sgu_w, sgu_b, conv_w, conv_b,
           lru_wa, lru_ba, lru_wx, lru_bx, lru_lambda, cd_w_out, cd_b_out, ln_mix_g, ln_mix_b, router_w, router_b,
           exp_w_gate, exp_w_up, exp_w_down, shared_w_gate, shared_w_up, shared_w_down, ln_ffn_g, ln_ffn_b):
    B, S, D = x.shape
    T = B * S
    tables = _position_tables(S)
    row = lambda v: v.reshape(1, -1)
    x2d = x.reshape(T, D)
    for layer in range(DEPTH):
        j = layer // 2
        if layer % 2 == 0:
            z = _inproj(x2d, ab_w_in[j].astype(BF16), row(ab_b_in[j]))
            y = _ab_core(z.reshape(B, S, -1), ab_sinks[j], tables)
            w_out, b_out = ab_w_out[j], ab_b_out[j]
        else:
            z = _inproj(x2d, cd_w_in[j].astype(BF16), row(cd_b_in[j]))
            y = _cd_core(z.reshape(B, S, -1), sgu_w[j], jnp.transpose(sgu_b[j]), conv_w[j], row(conv_b[j]),
                         lru_wa[j].astype(BF16), row(lru_ba[j]), lru_wx[j].astype(BF16), row(lru_bx[j]),
                         row(lru_lambda[j]))
            w_out, b_out = cd_w_out[j], cd_b_out[j]
        x2d, x_tiles = _outproj_ln(y.reshape(T, -1), w_out.astype(BF16), row(b_out), x2d,
                                   row(ln_mix_g[layer]), row(ln_mix_b[layer]))
        x2d = _moe_ln(x2d, x_tiles, router_w[layer], router_b[layer], exp_w_gate, exp_w_up, exp_w_down,
                      shared_w_gate[layer], shared_w_up[layer], shared_w_down[layer],
                      row(ln_ffn_g[layer]), row(ln_ffn_b[layer]), layer)
    return x2d.reshape(B, S, D)
```

```python
import functools
import math

import jax
import jax.numpy as jnp
from jax import lax
from jax.experimental import pallas as pl
from jax.experimental.pallas import tpu as pltpu

F32 = jnp.float32
BF16 = jnp.bfloat16
I32 = jnp.int32
U32 = jnp.uint32

CHUNK = 128
A_HEADS, A_KV_HEADS, A_HEAD_DIM = 16, 2, 64
R_HEADS, R_KEY_DIM, R_VAL_DIM = 4, 128, 256
SGU_GROUPS, SGU_GROUP_DIM = 4, 256
LRU_HEADS, LRU_BLOCK = 5, 256
CONV_WIDTH = 4
LRU_C = 8.0
TOP_K = 8
ROUTE_SCALE = 2.5
ROPE_THETA = 10000.0
LN_EPS = 1e-5
NEG_INF = -1e30
DEPTH = 4
DEEPNORM_ALPHA = (2 * DEPTH) ** 0.25

LANES = 128
PACKED_SUBLANES = 4
VMEM_LIMIT_BYTES = 48 * 1024 * 1024
EXPERTS_VMEM_LIMIT_BYTES = 56 * 1024 * 1024
ROW_TILE = 256
ROUTER_TILE = 512
EXPERT_ROWS = 256
COMBINE_TILE = 128


def _cparams(*sem):
    return pltpu.CompilerParams(dimension_semantics=sem, vmem_limit_bytes=VMEM_LIMIT_BYTES)


def _standardize(x):
    xc = x - jnp.mean(x, axis=-1, keepdims=True)
    return xc * lax.rsqrt(jnp.mean(xc * xc, axis=-1, keepdims=True) + LN_EPS)


def _dot(a, b):
    return jnp.dot(a, b, preferred_element_type=F32)


def _dot_nt(a, b):
    return lax.dot_general(a, b, (((1,), (1,)), ((), ())), preferred_element_type=F32)


def _inproj_body(x_ref, w_ref, b_ref, o_ref):
    o_ref[...] = _dot(x_ref[...].astype(BF16), w_ref[...]) + b_ref[...]


def _inproj(x2d, w_bf16, b_row):
    T, D = x2d.shape
    N = w_bf16.shape[1]
    return pl.pallas_call(
        _inproj_body,
        grid=(T // ROW_TILE,),
        in_specs=[pl.BlockSpec((ROW_TILE, D), lambda i: (i, 0)),
                  pl.BlockSpec((D, N), lambda i: (0, 0)),
                  pl.BlockSpec((1, N), lambda i: (0, 0))],
        out_specs=pl.BlockSpec((ROW_TILE, N), lambda i: (i, 0)),
        out_shape=jax.ShapeDtypeStruct((T, N), F32),
        compiler_params=_cparams("parallel"),
        name="inproj",
    )(x2d, w_bf16, b_row)


def _ab_body(sinks_ref, gch_ref, z_ref, cosa_ref, sina_ref, cosr_ref, sinr_ref, decay_ref, qdec_ref, kdec_ref,
             o_ref, kprev_ref, vprev_ref, state_ref):
    n = pl.program_id(1)
    C = CHUNK

    @pl.when(n == 0)
    def _():
        kprev_ref[...] = jnp.zeros_like(kprev_ref)
        vprev_ref[...] = jnp.zeros_like(vprev_ref)
        state_ref[...] = jnp.zeros_like(state_ref)

    lane = lax.broadcasted_iota(I32, (C, LANES), 1)
    lo_half = lane < 64
    first_32 = (lane & 32) == 0
    cosa = cosa_ref[...]
    sina = sina_ref[...]

    def rope64(x):
        swapped = jnp.where(first_32, pltpu.roll(x, 96, 1), pltpu.roll(x, 32, 1))
        return x * cosa + swapped * sina

    k = rope64(z_ref[0, :, 1024:1152])
    v = z_ref[0, :, 1152:1280]
    kw = jnp.concatenate([kprev_ref[...], k], axis=0)
    vw = jnp.concatenate([vprev_ref[...], v], axis=0)
    kprev_ref[...] = k
    vprev_ref[...] = v
    kw_sw = pltpu.roll(kw, 64, 1)
    vw_sw = pltpu.roll(vw, 64, 1)
    lane2 = lax.broadcasted_iota(I32, (2 * C, LANES), 1)
    lo2 = lane2 < 64

    rows = 8 * C
    qi = lax.broadcasted_iota(I32, (rows, 2 * C), 0) & (C - 1)
    kj = lax.broadcasted_iota(I32, (rows, 2 * C), 1)
    valid = (kj > qi) & (kj <= qi + C) & ((n > 0) | (kj >= C))
    rblk = lax.broadcasted_iota(I32, (rows, 1), 0) // C

    for g in range(A_KV_HEADS):
        if g == 0:
            kdup = jnp.where(lo2, kw, kw_sw)
            vlo = jnp.where(lo2, vw, 0.0)
            vhi = jnp.where(lo2, 0.0, vw_sw)
        else:
            kdup = jnp.where(lo2, kw_sw, kw)
            vlo = jnp.where(lo2, vw_sw, 0.0)
            vhi = jnp.where(lo2, 0.0, vw)
        qs = [rope64(z_ref[0, :, c * LANES:(c + 1) * LANES]) * (A_HEAD_DIM ** -0.5)
              for c in range(4 * g, 4 * g + 4)]
        qg = jnp.concatenate([jnp.where(lo_half, qc, 0.0) for qc in qs]
                             + [jnp.where(lo_half, 0.0, qc) for qc in qs], axis=0).astype(BF16)
        s = _dot_nt(qg, kdup.astype(BF16))
        s = jnp.where(valid, s, NEG_INF)
        sink = jnp.zeros((rows, 1), F32)
        for j in range(8):
            head = 8 * g + (2 * j if j < 4 else 2 * (j - 4) + 1)
            sink = jnp.where(rblk == j, sinks_ref[head], sink)
        m = jnp.maximum(jnp.max(s, axis=-1, keepdims=True), sink)
        e = jnp.exp(s - m)
        den = jnp.sum(e, axis=-1, keepdims=True) + jnp.exp(sink - m)
        p = (e * (1.0 / den)).astype(BF16)
        p2 = jnp.concatenate([p[:4 * C], p[4 * C:]], axis=1)
        vblk = jnp.concatenate([vlo, vhi], axis=0).astype(BF16)
        o = _dot(p2, vblk)
        for c in range(4):
            col = (4 * g + c) * LANES
            o_ref[0, :, col:col + LANES] = o[c * C:(c + 1) * C].astype(o_ref.dtype)

    cosr = cosr_ref[...]
    sinr = sinr_ref[...]

    def rope128(x):
        return x * cosr + pltpu.roll(x, 64, 1) * sinr

    q0, k0, v0, g0 = 1280, 1792, 2304, 3328
    for h in range(R_HEADS):
        qh = rope128(z_ref[0, :, q0 + h * 128:q0 + (h + 1) * 128])
        kh = rope128(z_ref[0, :, k0 + h * 128:k0 + (h + 1) * 128]) * (R_KEY_DIM ** -0.5)
        vb = z_ref[0, :, v0 + h * 256:v0 + (h + 1) * 256].astype(BF16)
        gh = z_ref[0, :, g0 + h * 256:g0 + (h + 1) * 256]
        s = _dot_nt(qh.astype(BF16), kh.astype(BF16)) * decay_ref[h]
        st = state_ref[h]
        o = _dot(s.astype(BF16), vb) + _dot((qh * qdec_ref[h]).astype(BF16), st.astype(BF16))
        kd_t = jnp.transpose(kh * kdec_ref[h]).astype(BF16)
        state_ref[h] = gch_ref[h] * st + _dot(kd_t, vb)
        y = jax.nn.silu(gh) * _standardize(o)
        col = 1024 + h * 256
        o_ref[0, :, col:col + 256] = y.astype(o_ref.dtype)


def _ab_core(z, sinks, tables):
    B, S, W = z.shape
    cosa, sina, cosr, sinr, decay, qdec, kdec, gch = tables
    smem = pl.BlockSpec(memory_space=pltpu.SMEM)
    tab = pl.BlockSpec((CHUNK, LANES), lambda b, n: (n, 0))
    const3 = pl.BlockSpec((R_HEADS, CHUNK, CHUNK), lambda b, n: (0, 0, 0))
    return pl.pallas_call(
        _ab_body,
        grid=(B, S // CHUNK),
        in_specs=[smem, smem,
                  pl.BlockSpec((1, CHUNK, W), lambda b, n: (b, n, 0)),
                  tab, tab, tab, tab, const3, const3, const3],
        out_specs=pl.BlockSpec((1, CHUNK, 2048), lambda b, n: (b, n, 0)),
        out_shape=jax.ShapeDtypeStruct((B, S, 2048), BF16),
        scratch_shapes=[pltpu.VMEM((CHUNK, LANES), F32), pltpu.VMEM((CHUNK, LANES), F32),
                        pltpu.VMEM((R_HEADS, R_KEY_DIM, R_VAL_DIM), F32)],
        compiler_params=_cparams("parallel", "arbitrary"),
        name="ab_core",
    )(sinks, gch, z, cosa, sina, cosr, sinr, decay, qdec, kdec)


def _cd_body(z_ref, sw_ref, sbt_ref, cw_ref, cb_ref, wa_ref, ba_ref, wx_ref, bx_ref, lam_ref,
             o_ref, xwin_ref, hc_ref):
    n = pl.program_id(1)
    C = CHUNK
    WL = LRU_HEADS * LRU_BLOCK

    @pl.when(n == 0)
    def _():
        xwin_ref[0:8, :] = jnp.zeros((8, WL), F32)
        hc_ref[...] = jnp.zeros_like(hc_ref)

    gu = jax.nn.gelu(z_ref[0, :, 0:1024])
    vn = _standardize(jax.nn.gelu(z_ref[0, :, 1024:2048])).astype(BF16)
    ri = lax.broadcasted_iota(I32, (C, C), 0)
    ci = lax.broadcasted_iota(I32, (C, C), 1)
    for g in range(SGU_GROUPS):
        w = jnp.where(ci <= ri, sw_ref[g], 0.0).astype(BF16)
        col = g * SGU_GROUP_DIM
        mixed = _dot(w, vn[:, col:col + SGU_GROUP_DIM]) + sbt_ref[:, g:g + 1]
        o_ref[0, :, col:col + SGU_GROUP_DIM] = (gu[:, col:col + SGU_GROUP_DIM] * mixed).astype(o_ref.dtype)

    xr = z_ref[0, :, 3328:3328 + WL]
    xwin_ref[8:8 + C, :] = xr
    xc = (cw_ref[3:4, :] * xr + cw_ref[2:3, :] * xwin_ref[7:7 + C, :] + cw_ref[1:2, :] * xwin_ref[6:6 + C, :]
          + cw_ref[0:1, :] * xwin_ref[5:5 + C, :] + cb_ref[...])
    xwin_ref[0:8, :] = xr[C - 8:C]
    xcb = xc.astype(BF16)
    ga, gi = [], []
    for h in range(LRU_HEADS):
        xb = xcb[:, h * LRU_BLOCK:(h + 1) * LRU_BLOCK]
        ga.append(_dot(xb, wa_ref[h]))
        gi.append(_dot(xb, wx_ref[h]))
    gate_r = jax.nn.sigmoid(jnp.concatenate(ga, axis=1) + ba_ref[...])
    gate_i = jax.nn.sigmoid(jnp.concatenate(gi, axis=1) + bx_ref[...])
    nl = -lam_ref[...]
    softplus = jnp.maximum(nl, 0.0) + jnp.log1p(jnp.exp(-jnp.abs(nl)))
    log_a = -LRU_C * gate_r * softplus
    a = jnp.exp(log_a)
    th = jnp.tanh(log_a)
    bv = jnp.sqrt(-2.0 * th / (1.0 - th)) * (gate_i * xc)
    r8 = lax.broadcasted_iota(I32, (C, WL), 0) & 7
    for d in (1, 2, 4):
        keep = r8 >= d
        a_sh = jnp.where(keep, pltpu.roll(a, d, 0), 1.0)
        b_sh = jnp.where(keep, pltpu.roll(bv, d, 0), 0.0)
        bv = a * b_sh + bv
        a = a * a_sh
    carry = hc_ref[...]
    hs = []
    for grp in range(C // 8):
        hg = bv[grp * 8:(grp + 1) * 8] + a[grp * 8:(grp + 1) * 8] * carry
        carry = hg[7:8]
        hs.append(hg)
    hc_ref[...] = carry
    hseq = jnp.concatenate(hs, axis=0)
    y = jax.nn.gelu(z_ref[0, :, 2048:2048 + WL]) * hseq
    o_ref[0, :, 1024:1024 + WL] = y.astype(o_ref.dtype)


def _cd_core(z, sgu_w, sgu_bt, conv_w, conv_b, wa, ba, wx, bx, lam):
    B, S, W = z.shape
    WL = LRU_HEADS * LRU_BLOCK
    full2 = lambda r, c: pl.BlockSpec((r, c), lambda b, n: (0, 0))
    full3 = lambda a, r, c: pl.BlockSpec((a, r, c), lambda b, n: (0, 0, 0))
    return pl.pallas_call(
        _cd_body,
        grid=(B, S // CHUNK),
        in_specs=[pl.BlockSpec((1, CHUNK, W), lambda b, n: (b, n, 0)),
                  full3(SGU_GROUPS, CHUNK, CHUNK), full2(CHUNK, SGU_GROUPS),
                  full2(CONV_WIDTH, WL), full2(1, WL),
                  full3(LRU_HEADS, LRU_BLOCK, LRU_BLOCK), full2(1, WL),
                  full3(LRU_HEADS, LRU_BLOCK, LRU_BLOCK), full2(1, WL), full2(1, WL)],
        out_specs=pl.BlockSpec((1, CHUNK, 1024 + WL), lambda b, n: (b, n, 0)),
        out_shape=jax.ShapeDtypeStruct((B, S, 1024 + WL), BF16),
        scratch_shapes=[pltpu.VMEM((8 + CHUNK, WL), F32), pltpu.VMEM((1, WL), F32)],
        compiler_params=_cparams("parallel", "arbitrary"),
        name="cd_core",
    )(z, sgu_w, sgu_bt, conv_w, conv_b, wa, ba, wx, bx, lam)


def _to_row_tiles(tiles_ref, val):
    n = val.shape[0]
    for j in range(val.shape[1] // LANES):
        tiles_ref[pl.ds(j, n, stride=8), :] = val[:, j * LANES:(j + 1) * LANES]


def _from_row_tiles(tiles_ref, n):
    return jnp.concatenate([tiles_ref[pl.ds(j, n, stride=8), :] for j in range(8)], axis=1)


def _pack_tiles(tiles):
    return pltpu.bitcast(tiles.astype(BF16), U32)


def _unpack_tiles(words):
    return pltpu.bitcast(words, BF16).astype(F32)


def _outproj_body(y_ref, w_ref, b_ref, x_ref, g_ref, beta_ref, o_ref, opacked_ref, tiles_ref):
    h = _dot(y_ref[...], w_ref[...]) + b_ref[...]
    xn = _standardize(DEEPNORM_ALPHA * x_ref[...] + h) * g_ref[...] + beta_ref[...]
    o_ref[...] = xn
    _to_row_tiles(tiles_ref, xn)
    opacked_ref[...] = _pack_tiles(tiles_ref[...])


def _outproj_ln(y2d, w_bf16, b_row, x2d, g_row, beta_row):
    T, Ky = y2d.shape
    D = x2d.shape[1]
    row = pl.BlockSpec((1, D), lambda i: (0, 0))
    return pl.pallas_call(
        _outproj_body,
        grid=(T // ROW_TILE,),
        in_specs=[pl.BlockSpec((ROW_TILE, Ky), lambda i: (i, 0)),
                  pl.BlockSpec((Ky, D), lambda i: (0, 0)), row,
                  pl.BlockSpec((ROW_TILE, D), lambda i: (i, 0)), row, row],
        out_specs=[pl.BlockSpec((ROW_TILE, D), lambda i: (i, 0)),
                   pl.BlockSpec((ROW_TILE * PACKED_SUBLANES, LANES), lambda i: (i, 0))],
        out_shape=[jax.ShapeDtypeStruct((T, D), F32), jax.ShapeDtypeStruct((T * PACKED_SUBLANES, LANES), U32)],
        scratch_shapes=[pltpu.VMEM((ROW_TILE * 8, LANES), F32)],
        compiler_params=_cparams("parallel"),
        name="outproj_ln",
    )(y2d, w_bf16, b_row, x2d, g_row, beta_row)


def _router_body(x_ref, wrt_ref, bcol_ref, idx_ref, gate_ref, pos_ref, cnt_ref, carry_ref):
    i = pl.program_id(0)
    E, tm = carry_ref.shape

    @pl.when(i == 0)
    def _():
        carry_ref[...] = jnp.zeros_like(carry_ref)

    logits = lax.dot_general(wrt_ref[...], x_ref[...], (((1,), (1,)), ((), ())),
                             precision=lax.Precision.HIGHEST, preferred_element_type=F32)
    scores = jax.nn.sigmoid(logits)
    sel = scores + bcol_ref[...]
    eio = lax.broadcasted_iota(I32, (E, tm), 0)
    onehots, idxs, tops = [], [], []
    for _ in range(TOP_K):
        m = jnp.max(sel, axis=0, keepdims=True)
        idx = jnp.min(jnp.where(sel == m, eio, E), axis=0, keepdims=True)
        hit = eio == idx
        tops.append(jnp.sum(jnp.where(hit, scores, 0.0), axis=0, keepdims=True))
        sel = jnp.where(hit, -jnp.inf, sel)
        onehots.append(hit)
        idxs.append(idx)
    chosen = onehots[0]
    for hit in onehots[1:]:
        chosen = chosen | hit
    mask = jnp.where(chosen, 1.0, 0.0).astype(BF16)
    ti = lax.broadcasted_iota(I32, (tm, tm), 0)
    tj = lax.broadcasted_iota(I32, (tm, tm), 1)
    earlier = jnp.where(ti < tj, 1.0, 0.0).astype(BF16)
    before = _dot(mask, earlier) + carry_ref[...]
    total = _dot(mask, jnp.ones((tm, tm), BF16))
    carry_ref[...] = carry_ref[...] + total
    cnt_ref[...] = carry_ref[:, 0:LANES]
    tsum = tops[0]
    for t in tops[1:]:
        tsum = tsum + t
    inv = ROUTE_SCALE / tsum
    gate_ref[...] = jnp.concatenate([t * inv for t in tops], axis=0)
    idx_ref[...] = jnp.concatenate(idxs, axis=0)
    pos_ref[...] = jnp.concatenate(
        [jnp.sum(jnp.where(hit, before, 0.0), axis=0, keepdims=True) for hit in onehots], axis=0).astype(I32)


def _router(x2d, wrt, bcol):
    T, D = x2d.shape
    E = wrt.shape[0]
    tm = ROUTER_TILE
    out8 = lambda dt: jax.ShapeDtypeStruct((TOP_K, T), dt)
    spec8 = pl.BlockSpec((TOP_K, tm), lambda i: (0, i))
    return pl.pallas_call(
        _router_body,
        grid=(T // tm,),
        in_specs=[pl.BlockSpec((tm, D), lambda i: (i, 0)),
                  pl.BlockSpec((E, D), lambda i: (0, 0)),
                  pl.BlockSpec((E, 1), lambda i: (0, 0))],
        out_specs=[spec8, spec8, spec8, pl.BlockSpec((E, LANES), lambda i: (0, 0))],
        out_shape=[out8(I32), out8(F32), out8(I32), jax.ShapeDtypeStruct((E, LANES), F32)],
        scratch_shapes=[pltpu.VMEM((E, tm), F32)],
        compiler_params=_cparams("arbitrary"),
        name="router",
    )(x2d, wrt, bcol)


def _experts_body(be_ref, nr_ref, rt_ref, x_hbm, wg_ref, wu_ref, wd_ref, y_ref,
                  xres_ref, stage_ref, tiles_ref, wgb_ref, wub_ref, wdb_ref, sem):
    i = pl.program_id(0)
    nreal = nr_ref[0]
    BM = EXPERT_ROWS
    P = PACKED_SUBLANES

    @pl.when(i == 0)
    def _():
        load_all = pltpu.make_async_copy(x_hbm, xres_ref, sem.at[0])
        load_all.start()
        load_all.wait()

    @pl.when(i < nreal)
    def _():
        @pl.when((i == 0) | (be_ref[i] != be_ref[jnp.maximum(i - 1, 0)]))
        def _():
            wgb_ref[...] = wg_ref[0].astype(BF16)
            wub_ref[...] = wu_ref[0].astype(BF16)
            wdb_ref[...] = wd_ref[0].astype(BF16)

        def copy_row(r, carry):
            t = rt_ref[0, 0, r]
            stage_ref[pl.ds(pl.multiple_of(r * P, P), P), :] = xres_ref[pl.ds(pl.multiple_of(t * P, P), P), :]
            return carry
        lax.fori_loop(0, BM, copy_row, 0, unroll=8)

        tiles_ref[...] = _unpack_tiles(stage_ref[...])
        xb = _from_row_tiles(tiles_ref, BM).astype(BF16)
        h = jax.nn.silu(_dot(xb, wgb_ref[...])) * _dot(xb, wub_ref[...])
        _to_row_tiles(tiles_ref, _dot(h.astype(BF16), wdb_ref[...]))
        y_ref[...] = _pack_tiles(tiles_ref[...])

    @pl.when(i >= nreal)
    def _():
        y_ref[...] = _pack_tiles(jnp.zeros((BM * 8, LANES), F32))


def _experts(x_packed, row_tok, block_e, nreal, w_gate, w_up, w_down, layer):
    _, E, D, H = w_gate.shape
    NB = block_e.shape[0]
    BM = EXPERT_ROWS
    P = PACKED_SUBLANES
    rt3 = row_tok.reshape(NB, 1, BM)
    grid_spec = pltpu.PrefetchScalarGridSpec(
        num_scalar_prefetch=2,
        grid=(NB,),
        in_specs=[pl.BlockSpec((1, 1, BM), lambda i, be, nr: (i, 0, 0), memory_space=pltpu.SMEM),
                  pl.BlockSpec(memory_space=pl.ANY),
                  pl.BlockSpec((None, 1, D, H), lambda i, be, nr: (layer, be[i], 0, 0)),
                  pl.BlockSpec((None, 1, D, H), lambda i, be, nr: (layer, be[i], 0, 0)),
                  pl.BlockSpec((None, 1, H, D), lambda i, be, nr: (layer, be[i], 0, 0))],
        out_specs=pl.BlockSpec((BM * P, LANES), lambda i, be, nr: (i, 0)),
        scratch_shapes=[pltpu.VMEM(x_packed.shape, U32), pltpu.VMEM((BM * P, LANES), U32),
                        pltpu.VMEM((BM * 8, LANES), F32), pltpu.VMEM((D, H), BF16), pltpu.VMEM((D, H), BF16), pltpu.VMEM((H, D), BF16),
                        pltpu.SemaphoreType.DMA((1,))],
    )
    return pl.pallas_call(
        _experts_body,
        grid_spec=grid_spec,
        out_shape=jax.ShapeDtypeStruct((NB * BM * P, LANES), U32),
        compiler_params=pltpu.CompilerParams(dimension_semantics=("arbitrary",),
                                             vmem_limit_bytes=EXPERTS_VMEM_LIMIT_BYTES),
        name="experts",
    )(block_e, nreal, rt3, x_packed, w_gate, w_up, w_down)


def _row_gather_start(idx_of, n_rows, src_hbm, dst, sem):
    P = PACKED_SUBLANES

    def body(p, carry):
        for u in range(2):
            r = 2 * p + u
            pltpu.make_async_copy(src_hbm.at[pl.ds(pl.multiple_of(idx_of(r) * P, P), P), :],
                                  dst.at[pl.ds(pl.multiple_of(r * P, P), P), :], sem).start(priority=u)
        return carry
    lax.fori_loop(0, n_rows // 2, body, 0, unroll=4)


def _row_gather_wait(src_hbm, dst, sem):
    pltpu.make_async_copy(src_hbm.at[pl.ds(0, dst.shape[0]), :], dst, sem).wait()


def _combine_body(d_cur_ref, d_nxt_ref, g_ref, x_ref, y_hbm, wsg_ref, wsu_ref, wsd_ref, lng_ref, lnb_ref,
                  o_ref, buf_ref, tiles_ref, sem):
    i = pl.program_id(0)
    nt = pl.num_programs(0)
    slot = i % 2
    tm = COMBINE_TILE

    def start(d_ref, s):
        _row_gather_start(lambda r: d_ref[0, 0, r], TOP_K * tm, y_hbm, buf_ref.at[s], sem.at[s])

    @pl.when(i == 0)
    def _():
        start(d_cur_ref, 0)

    @pl.when(i + 1 < nt)
    def _():
        start(d_nxt_ref, 1 - slot)

    x = x_ref[...]
    xb = x.astype(BF16)
    hs = jax.nn.silu(_dot(xb, wsg_ref[...])) * _dot(xb, wsu_ref[...])
    acc = _dot(hs.astype(BF16), wsd_ref[...])
    _row_gather_wait(y_hbm, buf_ref.at[slot], sem.at[slot])
    g = g_ref[...]
    rows = tm * PACKED_SUBLANES
    mix = g[:, 0:1] * _unpack_tiles(buf_ref[slot, 0:rows, :])
    for k in range(1, TOP_K):
        mix = mix + g[:, k:k + 1] * _unpack_tiles(buf_ref[slot, k * rows:(k + 1) * rows, :])
    tiles_ref[...] = mix
    routed = _from_row_tiles(tiles_ref, tm)
    o_ref[...] = _standardize(DEEPNORM_ALPHA * x + (routed + acc)) * lng_ref[...] + lnb_ref[...]


def _combine_ln(x2d, y_tiles, dest_tiles, gates, wsg, wsu, wsd, g_row, beta_row):
    T, D = x2d.shape
    H = wsg.shape[1]
    tm = COMBINE_TILE
    nt = T // tm
    row = pl.BlockSpec((1, D), lambda i: (0, 0))
    return pl.pallas_call(
        _combine_body,
        grid=(nt,),
        in_specs=[pl.BlockSpec((1, 1, TOP_K * tm), lambda i: (i, 0, 0), memory_space=pltpu.SMEM),
                  pl.BlockSpec((1, 1, TOP_K * tm), lambda i: (jnp.minimum(i + 1, nt - 1), 0, 0),
                               memory_space=pltpu.SMEM),
                  pl.BlockSpec((tm * 8, TOP_K), lambda i: (i, 0)),
                  pl.BlockSpec((tm, D), lambda i: (i, 0)),
                  pl.BlockSpec(memory_space=pl.ANY),
                  pl.BlockSpec((D, H), lambda i: (0, 0)),
                  pl.BlockSpec((D, H), lambda i: (0, 0)),
                  pl.BlockSpec((H, D), lambda i: (0, 0)), row, row],
        out_specs=pl.BlockSpec((tm, D), lambda i: (i, 0)),
        out_shape=jax.ShapeDtypeStruct((T, D), F32),
        scratch_shapes=[pltpu.VMEM((2, TOP_K * tm * PACKED_SUBLANES, LANES), U32),
                        pltpu.VMEM((tm * 8, LANES), F32), pltpu.SemaphoreType.DMA((2,))],
        compiler_params=_cparams("arbitrary"),
        name="combine_ln",
    )(dest_tiles, dest_tiles, gates, x2d, y_tiles, wsg, wsu, wsd, g_row, beta_row)


def _moe_ln(x2d, x_tiles, w_router, b_router, w_gate, w_up, w_down, ws_gate, ws_up, ws_down, g_row, beta_row,
            layer):
    T, D = x2d.shape
    E = w_router.shape[1]
    BM = EXPERT_ROWS
    NB = (T * TOP_K) // BM + E
    idx_t, gate_t, pos_t, cnt = _router(x2d, jnp.transpose(w_router), b_router.reshape(E, 1))
    counts = cnt[:, 0].astype(I32)
    padded = (counts + BM - 1) // BM * BM
    pend = jnp.cumsum(padded)
    pstart = pend - padded
    nreal = (pend[-1] // BM).astype(I32).reshape(1)
    block_start = jnp.arange(NB, dtype=I32) * BM
    block_e = jnp.minimum(jnp.sum((pend[None, :] <= block_start[:, None]).astype(I32), axis=1), E - 1)
    eids = jnp.arange(E, dtype=I32)
    dest_t = jnp.sum(jnp.where(idx_t[:, :, None] == eids, pstart, 0), axis=-1) + pos_t
    span = 2 * T
    tok = jnp.arange(T, dtype=I32)
    real_keys = (idx_t * span + tok[None, :]).reshape(-1)
    j = jnp.arange(BM, dtype=I32)
    pad_keys = jnp.where(j[None, :] < (padded - counts)[:, None], eids[:, None] * span + T + j[None, :],
                         jnp.iinfo(jnp.int32).max).reshape(-1)
    keys = jnp.sort(jnp.concatenate([real_keys, pad_keys]))
    low = keys % span
    row_tok = jnp.where((keys < E * span) & (low < T), low, 0)
    y_tiles = _experts(x_tiles, row_tok, block_e, nreal, w_gate, w_up, w_down, layer)
    nt = T // COMBINE_TILE
    dest_tiles = jnp.transpose(dest_t.reshape(TOP_K, nt, COMBINE_TILE), (1, 0, 2)).reshape(nt, 1, -1)
    gates_on_tile_rows = jnp.repeat(jnp.transpose(gate_t), 8, axis=0)
    return _combine_ln(x2d, y_tiles, dest_tiles, gates_on_tile_rows, ws_gate.astype(BF16),
                       ws_up.astype(BF16), ws_down.astype(BF16), g_row, beta_row)


def _position_tables(S):
    pos = jnp.arange(S, dtype=F32)[:, None]
    attn_freq = ROPE_THETA ** (-jnp.arange(0, A_HEAD_DIM, 2, dtype=F32) / A_HEAD_DIM)
    ang = pos * attn_freq[None, :]
    cosa = jnp.tile(jnp.cos(ang), (1, 4))
    sina = jnp.tile(jnp.concatenate([-jnp.sin(ang), jnp.sin(ang)], axis=1), (1, 2))
    ret_freq = ROPE_THETA ** (-jnp.linspace(0.0, 1.0, R_KEY_DIM // 2, dtype=F32))
    angr = pos * ret_freq[None, :]
    cosr = jnp.tile(jnp.cos(angr), (1, 2))
    sinr = jnp.concatenate([-jnp.sin(angr), jnp.sin(angr)], axis=1)
    C = CHUNK
    log_g = jnp.log1p(-jnp.exp2(-5.0 - jnp.arange(R_HEADS, dtype=F32)))
    idx = jnp.arange(C, dtype=F32)
    diff = idx[:, None] - idx[None, :]
    decay = jnp.where(diff >= 0, jnp.exp(log_g[:, None, None] * jnp.maximum(diff, 0.0)), 0.0)
    kdec = jnp.exp(log_g[:, None] * (C - 1.0 - idx)[None, :])
    qdec = jnp.exp(log_g[:, None] * (idx + 1.0)[None, :])
    bc = lambda t: jnp.broadcast_to(t[:, :, None], (R_HEADS, C, R_KEY_DIM))
    gch = jnp.exp(log_g * C)
    return cosa, sina, cosr, sinr, decay, bc(qdec), bc(kdec), gch


def kernel(x, ab_w_in, ab_b_in, ab_sinks, ab_w_out, ab_b_out, cd_w_in, cd_b_in, sgu_w, sgu_b, conv_w, conv_b,
           lru_wa, lru_ba, lru_wx, lru_bx, lru_lambda, cd_w_out, cd_b_out, ln_mix_g, ln_mix_b, router_w, router_b,
           exp_w_gate, exp_w_up, exp_w_down, shared_w_gate, shared_w_up, shared_w_down, ln_ffn_g, ln_ffn_b):
    B, S, D = x.shape
    T = B * S
    tables = _position_tables(S)
    row = lambda v: v.reshape(1, -1)
    x2d = x.reshape(T, D)
    for layer in range(DEPTH):
        j = layer // 2
        if layer % 2 == 0:
            z = _inproj(x2d, ab_w_in[j].astype(BF16), row(ab_b_in[j]))
            y = _ab_core(z.reshape(B, S, -1), ab_sinks[j], tables)
            w_out, b_out = ab_w_out[j], ab_b_out[j]
        else:
            z = _inproj(x2d, cd_w_in[j].astype(BF16), row(cd_b_in[j]))
            y = _cd_core(z.reshape(B, S, -1), sgu_w[j], jnp.transpose(sgu_b[j]), conv_w[j], row(conv_b[j]),
                         lru_wa[j].astype(BF16), row(lru_ba[j]), lru_wx[j].astype(BF16), row(lru_bx[j]),
                         row(lru_lambda[j]))
            w_out, b_out = cd_w_out[j], cd_b_out[j]
        x2d, x_tiles = _outproj_ln(y.reshape(T, -1), w_out.astype(BF16), row(b_out), x2d,
                                   row(ln_mix_g[layer]), row(ln_mix_b[layer]))
        x2d = _moe_ln(x2d, x_tiles, router_w[layer], router_b[layer], exp_w_gate, exp_w_up, exp_w_down,
                      shared_w_gate[layer], shared_w_up[layer], shared_w_down[layer],
                      row(ln_ffn_g[layer]), row(ln_ffn_b[layer]), layer)
    return x2d.reshape(B, S, D)
```

```python
import functools
import math

import jax
import jax.numpy as jnp
from jax import lax
from jax.experimental import pallas as pl
from jax.experimental.pallas import tpu as pltpu

F32 = jnp.float32
BF16 = jnp.bfloat16
I32 = jnp.int32
U32 = jnp.uint32

CHUNK = 128
A_HEADS, A_KV_HEADS, A_HEAD_DIM = 16, 2, 64
R_HEADS, R_KEY_DIM, R_VAL_DIM = 4, 128, 256
SGU_GROUPS, SGU_GROUP_DIM = 4, 256
LRU_HEADS, LRU_BLOCK = 5, 256
CONV_WIDTH = 4
LRU_C = 8.0
TOP_K = 8
ROUTE_SCALE = 2.5
ROPE_THETA = 10000.0
LN_EPS = 1e-5
NEG_INF = -1e30
DEPTH = 4
DEEPNORM_ALPHA = (2 * DEPTH) ** 0.25

LANES = 128
PACKED_SUBLANES = 4
VMEM_LIMIT_BYTES = 48 * 1024 * 1024
EXPERTS_VMEM_LIMIT_BYTES = 56 * 1024 * 1024
ROW_TILE = 256
ROUTER_TILE = 512
EXPERT_ROWS = 512
COMBINE_TILE = 128


def _cparams(*sem):
    return pltpu.CompilerParams(dimension_semantics=sem, vmem_limit_bytes=VMEM_LIMIT_BYTES)


def _standardize(x):
    xc = x - jnp.mean(x, axis=-1, keepdims=True)
    return xc * lax.rsqrt(jnp.mean(xc * xc, axis=-1, keepdims=True) + LN_EPS)


def _dot(a, b):
    return jnp.dot(a, b, preferred_element_type=F32)


def _dot_nt(a, b):
    return lax.dot_general(a, b, (((1,), (1,)), ((), ())), preferred_element_type=F32)


def _inproj_body(x_ref, w_ref, b_ref, o_ref):
    o_ref[...] = _dot(x_ref[...].astype(BF16), w_ref[...]) + b_ref[...]


def _inproj(x2d, w_bf16, b_row):
    T, D = x2d.shape
    N = w_bf16.shape[1]
    return pl.pallas_call(
        _inproj_body,
        grid=(T // ROW_TILE,),
        in_specs=[pl.BlockSpec((ROW_TILE, D), lambda i: (i, 0)),
                  pl.BlockSpec((D, N), lambda i: (0, 0)),
                  pl.BlockSpec((1, N), lambda i: (0, 0))],
        out_specs=pl.BlockSpec((ROW_TILE, N), lambda i: (i, 0)),
        out_shape=jax.ShapeDtypeStruct((T, N), F32),
        compiler_params=_cparams("parallel"),
        name="inproj",
    )(x2d, w_bf16, b_row)


def _ab_body(sinks_ref, gch_ref, z_ref, cosa_ref, sina_ref, cosr_ref, sinr_ref, decay_ref, qdec_ref, kdec_ref,
             o_ref, kprev_ref, vprev_ref, state_ref):
    n = pl.program_id(1)
    C = CHUNK

    @pl.when(n == 0)
    def _():
        kprev_ref[...] = jnp.zeros_like(kprev_ref)
        vprev_ref[...] = jnp.zeros_like(vprev_ref)
        state_ref[...] = jnp.zeros_like(state_ref)

    lane = lax.broadcasted_iota(I32, (C, LANES), 1)
    lo_half = lane < 64
    first_32 = (lane & 32) == 0
    cosa = cosa_ref[...]
    sina = sina_ref[...]

    def rope64(x):
        swapped = jnp.where(first_32, pltpu.roll(x, 96, 1), pltpu.roll(x, 32, 1))
        return x * cosa + swapped * sina

    k = rope64(z_ref[0, :, 1024:1152])
    v = z_ref[0, :, 1152:1280]
    kw = jnp.concatenate([kprev_ref[...], k], axis=0)
    vw = jnp.concatenate([vprev_ref[...], v], axis=0)
    kprev_ref[...] = k
    vprev_ref[...] = v
    kw_sw = pltpu.roll(kw, 64, 1)
    vw_sw = pltpu.roll(vw, 64, 1)
    lane2 = lax.broadcasted_iota(I32, (2 * C, LANES), 1)
    lo2 = lane2 < 64

    rows = 8 * C
    qi = lax.broadcasted_iota(I32, (rows, 2 * C), 0) & (C - 1)
    kj = lax.broadcasted_iota(I32, (rows, 2 * C), 1)
    valid = (kj > qi) & (kj <= qi + C) & ((n > 0) | (kj >= C))
    rblk = lax.broadcasted_iota(I32, (rows, 1), 0) // C

    for g in range(A_KV_HEADS):
        if g == 0:
            kdup = jnp.where(lo2, kw, kw_sw)
            vlo = jnp.where(lo2, vw, 0.0)
            vhi = jnp.where(lo2, 0.0, vw_sw)
        else:
            kdup = jnp.where(lo2, kw_sw, kw)
            vlo = jnp.where(lo2, vw_sw, 0.0)
            vhi = jnp.where(lo2, 0.0, vw)
        qs = [rope64(z_ref[0, :, c * LANES:(c + 1) * LANES]) * (A_HEAD_DIM ** -0.5)
              for c in range(4 * g, 4 * g + 4)]
        qg = jnp.concatenate([jnp.where(lo_half, qc, 0.0) for qc in qs]
                             + [jnp.where(lo_half, 0.0, qc) for qc in qs], axis=0).astype(BF16)
        s = _dot_nt(qg, kdup.astype(BF16))
        s = jnp.where(valid, s, NEG_INF)
        sink = jnp.zeros((rows, 1), F32)
        for j in range(8):
            head = 8 * g + (2 * j if j < 4 else 2 * (j - 4) + 1)
            sink = jnp.where(rblk == j, sinks_ref[head], sink)
        m = jnp.maximum(jnp.max(s, axis=-1, keepdims=True), sink)
        e = jnp.exp(s - m)
        den = jnp.sum(e, axis=-1, keepdims=True) + jnp.exp(sink - m)
        p = (e * (1.0 / den)).astype(BF16)
        p2 = jnp.concatenate([p[:4 * C], p[4 * C:]], axis=1)
        vblk = jnp.concatenate([vlo, vhi], axis=0).astype(BF16)
        o = _dot(p2, vblk)
        for c in range(4):
            col = (4 * g + c) * LANES
            o_ref[0, :, col:col + LANES] = o[c * C:(c + 1) * C].astype(o_ref.dtype)

    cosr = cosr_ref[...]
    sinr = sinr_ref[...]

    def rope128(x):
        return x * cosr + pltpu.roll(x, 64, 1) * sinr

    q0, k0, v0, g0 = 1280, 1792, 2304, 3328
    for h in range(R_HEADS):
        qh = rope128(z_ref[0, :, q0 + h * 128:q0 + (h + 1) * 128])
        kh = rope128(z_ref[0, :, k0 + h * 128:k0 + (h + 1) * 128]) * (R_KEY_DIM ** -0.5)
        vb = z_ref[0, :, v0 + h * 256:v0 + (h + 1) * 256].astype(BF16)
        gh = z_ref[0, :, g0 + h * 256:g0 + (h + 1) * 256]
        s = _dot_nt(qh.astype(BF16), kh.astype(BF16)) * decay_ref[h]
        st = state_ref[h]
        o = _dot(s.astype(BF16), vb) + _dot((qh * qdec_ref[h]).astype(BF16), st.astype(BF16))
        kd_t = jnp.transpose(kh * kdec_ref[h]).astype(BF16)
        state_ref[h] = gch_ref[h] * st + _dot(kd_t, vb)
        y = jax.nn.silu(gh) * _standardize(o)
        col = 1024 + h * 256
        o_ref[0, :, col:col + 256] = y.astype(o_ref.dtype)


def _ab_core(z, sinks, tables):
    B, S, W = z.shape
    cosa, sina, cosr, sinr, decay, qdec, kdec, gch = tables
    smem = pl.BlockSpec(memory_space=pltpu.SMEM)
    tab = pl.BlockSpec((CHUNK, LANES), lambda b, n: (n, 0))
    const3 = pl.BlockSpec((R_HEADS, CHUNK, CHUNK), lambda b, n: (0, 0, 0))
    return pl.pallas_call(
        _ab_body,
        grid=(B, S // CHUNK),
        in_specs=[smem, smem,
                  pl.BlockSpec((1, CHUNK, W), lambda b, n: (b, n, 0)),
                  tab, tab, tab, tab, const3, const3, const3],
        out_specs=pl.BlockSpec((1, CHUNK, 2048), lambda b, n: (b, n, 0)),
        out_shape=jax.ShapeDtypeStruct((B, S, 2048), BF16),
        scratch_shapes=[pltpu.VMEM((CHUNK, LANES), F32), pltpu.VMEM((CHUNK, LANES), F32),
                        pltpu.VMEM((R_HEADS, R_KEY_DIM, R_VAL_DIM), F32)],
        compiler_params=_cparams("parallel", "arbitrary"),
        name="ab_core",
    )(sinks, gch, z, cosa, sina, cosr, sinr, decay, qdec, kdec)


def _cd_body(z_ref, sw_ref, sbt_ref, cw_ref, cb_ref, wa_ref, ba_ref, wx_ref, bx_ref, lam_ref,
             o_ref, xwin_ref, hc_ref):
    n = pl.program_id(1)
    C = CHUNK
    WL = LRU_HEADS * LRU_BLOCK

    @pl.when(n == 0)
    def _():
        xwin_ref[0:8, :] = jnp.zeros((8, WL), F32)
        hc_ref[...] = jnp.zeros_like(hc_ref)

    gu = jax.nn.gelu(z_ref[0, :, 0:1024])
    vn = _standardize(jax.nn.gelu(z_ref[0, :, 1024:2048])).astype(BF16)
    ri = lax.broadcasted_iota(I32, (C, C), 0)
    ci = lax.broadcasted_iota(I32, (C, C), 1)
    for g in range(SGU_GROUPS):
        w = jnp.where(ci <= ri, sw_ref[g], 0.0).astype(BF16)
        col = g * SGU_GROUP_DIM
        mixed = _dot(w, vn[:, col:col + SGU_GROUP_DIM]) + sbt_ref[:, g:g + 1]
        o_ref[0, :, col:col + SGU_GROUP_DIM] = (gu[:, col:col + SGU_GROUP_DIM] * mixed).astype(o_ref.dtype)

    xr = z_ref[0, :, 3328:3328 + WL]
    xwin_ref[8:8 + C, :] = xr
    xc = (cw_ref[3:4, :] * xr + cw_ref[2:3, :] * xwin_ref[7:7 + C, :] + cw_ref[1:2, :] * xwin_ref[6:6 + C, :]
          + cw_ref[0:1, :] * xwin_ref[5:5 + C, :] + cb_ref[...])
    xwin_ref[0:8, :] = xr[C - 8:C]
    xcb = xc.astype(BF16)
    ga, gi = [], []
    for h in range(LRU_HEADS):
        xb = xcb[:, h * LRU_BLOCK:(h + 1) * LRU_BLOCK]
        ga.append(_dot(xb, wa_ref[h]))
        gi.append(_dot(xb, wx_ref[h]))
    gate_r = jax.nn.sigmoid(jnp.concatenate(ga, axis=1) + ba_ref[...])
    gate_i = jax.nn.sigmoid(jnp.concatenate(gi, axis=1) + bx_ref[...])
    nl = -lam_ref[...]
    softplus = jnp.maximum(nl, 0.0) + jnp.log1p(jnp.exp(-jnp.abs(nl)))
    log_a = -LRU_C * gate_r * softplus
    a = jnp.exp(log_a)
    th = jnp.tanh(log_a)
    bv = jnp.sqrt(-2.0 * th / (1.0 - th)) * (gate_i * xc)
    r8 = lax.broadcasted_iota(I32, (C, WL), 0) & 7
    for d in (1, 2, 4):
        keep = r8 >= d
        a_sh = jnp.where(keep, pltpu.roll(a, d, 0), 1.0)
        b_sh = jnp.where(keep, pltpu.roll(bv, d, 0), 0.0)
        bv = a * b_sh + bv
        a = a * a_sh
    carry = hc_ref[...]
    hs = []
    for grp in range(C // 8):
        hg = bv[grp * 8:(grp + 1) * 8] + a[grp * 8:(grp + 1) * 8] * carry
        carry = hg[7:8]
        hs.append(hg)
    hc_ref[...] = carry
    hseq = jnp.concatenate(hs, axis=0)
    y = jax.nn.gelu(z_ref[0, :, 2048:2048 + WL]) * hseq
    o_ref[0, :, 1024:1024 + WL] = y.astype(o_ref.dtype)


def _cd_core(z, sgu_w, sgu_bt, conv_w, conv_b, wa, ba, wx, bx, lam):
    B, S, W = z.shape
    WL = LRU_HEADS * LRU_BLOCK
    full2 = lambda r, c: pl.BlockSpec((r, c), lambda b, n: (0, 0))
    full3 = lambda a, r, c: pl.BlockSpec((a, r, c), lambda b, n: (0, 0, 0))
    return pl.pallas_call(
        _cd_body,
        grid=(B, S // CHUNK),
        in_specs=[pl.BlockSpec((1, CHUNK, W), lambda b, n: (b, n, 0)),
                  full3(SGU_GROUPS, CHUNK, CHUNK), full2(CHUNK, SGU_GROUPS),
                  full2(CONV_WIDTH, WL), full2(1, WL),
                  full3(LRU_HEADS, LRU_BLOCK, LRU_BLOCK), full2(1, WL),
                  full3(LRU_HEADS, LRU_BLOCK, LRU_BLOCK), full2(1, WL), full2(1, WL)],
        out_specs=pl.BlockSpec((1, CHUNK, 1024 + WL), lambda b, n: (b, n, 0)),
        out_shape=jax.ShapeDtypeStruct((B, S, 1024 + WL), BF16),
        scratch_shapes=[pltpu.VMEM((8 + CHUNK, WL), F32), pltpu.VMEM((1, WL), F32)],
        compiler_params=_cparams("parallel", "arbitrary"),
        name="cd_core",
    )(z, sgu_w, sgu_bt, conv_w, conv_b, wa, ba, wx, bx, lam)


def _to_row_tiles(tiles_ref, val):
    n = val.shape[0]
    for j in range(val.shape[1] // LANES):
        tiles_ref[pl.ds(j, n, stride=8), :] = val[:, j * LANES:(j + 1) * LANES]


def _from_row_tiles(tiles_ref, n):
    return jnp.concatenate([tiles_ref[pl.ds(j, n, stride=8), :] for j in range(8)], axis=1)


def _pack_tiles(tiles):
    return pltpu.bitcast(tiles.astype(BF16), U32)


def _unpack_tiles(words):
    return pltpu.bitcast(words, BF16).astype(F32)


def _outproj_body(y_ref, w_ref, b_ref, x_ref, g_ref, beta_ref, o_ref, opacked_ref, tiles_ref):
    h = _dot(y_ref[...], w_ref[...]) + b_ref[...]
    xn = _standardize(DEEPNORM_ALPHA * x_ref[...] + h) * g_ref[...] + beta_ref[...]
    o_ref[...] = xn
    _to_row_tiles(tiles_ref, xn)
    opacked_ref[...] = _pack_tiles(tiles_ref[...])


def _outproj_ln(y2d, w_bf16, b_row, x2d, g_row, beta_row):
    T, Ky = y2d.shape
    D = x2d.shape[1]
    row = pl.BlockSpec((1, D), lambda i: (0, 0))
    return pl.pallas_call(
        _outproj_body,
        grid=(T // ROW_TILE,),
        in_specs=[pl.BlockSpec((ROW_TILE, Ky), lambda i: (i, 0)),
                  pl.BlockSpec((Ky, D), lambda i: (0, 0)), row,
                  pl.BlockSpec((ROW_TILE, D), lambda i: (i, 0)), row, row],
        out_specs=[pl.BlockSpec((ROW_TILE, D), lambda i: (i, 0)),
                   pl.BlockSpec((ROW_TILE * PACKED_SUBLANES, LANES), lambda i: (i, 0))],
        out_shape=[jax.ShapeDtypeStruct((T, D), F32), jax.ShapeDtypeStruct((T * PACKED_SUBLANES, LANES), U32)],
        scratch_shapes=[pltpu.VMEM((ROW_TILE * 8, LANES), F32)],
        compiler_params=_cparams("parallel"),
        name="outproj_ln",
    )(y2d, w_bf16, b_row, x2d, g_row, beta_row)


def _router_body(x_ref, wrt_ref, bcol_ref, idx_ref, gate_ref, pos_ref, cnt_ref, carry_ref):
    i = pl.program_id(0)
    E, tm = carry_ref.shape

    @pl.when(i == 0)
    def _():
        carry_ref[...] = jnp.zeros_like(carry_ref)

    logits = lax.dot_general(wrt_ref[...], x_ref[...], (((1,), (1,)), ((), ())),
                             precision=lax.Precision.HIGHEST, preferred_element_type=F32)
    scores = jax.nn.sigmoid(logits)
    sel = scores + bcol_ref[...]
    eio = lax.broadcasted_iota(I32, (E, tm), 0)
    onehots, idxs, tops = [], [], []
    for _ in range(TOP_K):
        m = jnp.max(sel, axis=0, keepdims=True)
        idx = jnp.min(jnp.where(sel == m, eio, E), axis=0, keepdims=True)
        hit = eio == idx
        tops.append(jnp.sum(jnp.where(hit, scores, 0.0), axis=0, keepdims=True))
        sel = jnp.where(hit, -jnp.inf, sel)
        onehots.append(hit)
        idxs.append(idx)
    chosen = onehots[0]
    for hit in onehots[1:]:
        chosen = chosen | hit
    mask = jnp.where(chosen, 1.0, 0.0).astype(BF16)
    ti = lax.broadcasted_iota(I32, (tm, tm), 0)
    tj = lax.broadcasted_iota(I32, (tm, tm), 1)
    earlier = jnp.where(ti < tj, 1.0, 0.0).astype(BF16)
    before = _dot(mask, earlier) + carry_ref[...]
    total = _dot(mask, jnp.ones((tm, tm), BF16))
    carry_ref[...] = carry_ref[...] + total
    cnt_ref[...] = carry_ref[:, 0:LANES]
    tsum = tops[0]
    for t in tops[1:]:
        tsum = tsum + t
    inv = ROUTE_SCALE / tsum
    gate_ref[...] = jnp.concatenate([t * inv for t in tops], axis=0)
    idx_ref[...] = jnp.concatenate(idxs, axis=0)
    pos_ref[...] = jnp.concatenate(
        [jnp.sum(jnp.where(hit, before, 0.0), axis=0, keepdims=True) for hit in onehots], axis=0).astype(I32)


def _router(x2d, wrt, bcol):
    T, D = x2d.shape
    E = wrt.shape[0]
    tm = ROUTER_TILE
    out8 = lambda dt: jax.ShapeDtypeStruct((TOP_K, T), dt)
    spec8 = pl.BlockSpec((TOP_K, tm), lambda i: (0, i))
    return pl.pallas_call(
        _router_body,
        grid=(T // tm,),
        in_specs=[pl.BlockSpec((tm, D), lambda i: (i, 0)),
                  pl.BlockSpec((E, D), lambda i: (0, 0)),
                  pl.BlockSpec((E, 1), lambda i: (0, 0))],
        out_specs=[spec8, spec8, spec8, pl.BlockSpec((E, LANES), lambda i: (0, 0))],
        out_shape=[out8(I32), out8(F32), out8(I32), jax.ShapeDtypeStruct((E, LANES), F32)],
        scratch_shapes=[pltpu.VMEM((E, tm), F32)],
        compiler_params=_cparams("arbitrary"),
        name="router",
    )(x2d, wrt, bcol)


def _experts_body(be_ref, nr_ref, rt_ref, rg_ref, x_hbm, wg_ref, wu_ref, wd_ref, y_ref,
                  xres_ref, stage_ref, tiles_ref, wgb_ref, wub_ref, wdb_ref, sem):
    i = pl.program_id(0)
    nreal = nr_ref[0]
    BM = EXPERT_ROWS
    P = PACKED_SUBLANES

    @pl.when(i == 0)
    def _():
        load_all = pltpu.make_async_copy(x_hbm, xres_ref, sem.at[0])
        load_all.start()
        load_all.wait()

    @pl.when(i < nreal)
    def _():
        @pl.when((i == 0) | (be_ref[i] != be_ref[jnp.maximum(i - 1, 0)]))
        def _():
            wgb_ref[...] = wg_ref[0].astype(BF16)
            wub_ref[...] = wu_ref[0].astype(BF16)
            wdb_ref[...] = wd_ref[0].astype(BF16)

        def copy_row(r, carry):
            t = rt_ref[0, 0, r]
            stage_ref[pl.ds(pl.multiple_of(r * P, P), P), :] = xres_ref[pl.ds(pl.multiple_of(t * P, P), P), :]
            return carry
        lax.fori_loop(0, BM, copy_row, 0, unroll=8)

        tiles_ref[...] = _unpack_tiles(stage_ref[...])
        xb = _from_row_tiles(tiles_ref, BM).astype(BF16)
        h = jax.nn.silu(_dot(xb, wgb_ref[...])) * _dot(xb, wub_ref[...])
        _to_row_tiles(tiles_ref, _dot(h.astype(BF16), wdb_ref[...]) * rg_ref[...])
        y_ref[...] = _pack_tiles(tiles_ref[...])

    @pl.when(i >= nreal)
    def _():
        y_ref[...] = _pack_tiles(jnp.zeros((BM * 8, LANES), F32))


def _experts(x_packed, row_tok, row_gate, block_e, nreal, w_gate, w_up, w_down, layer):
    _, E, D, H = w_gate.shape
    NB = block_e.shape[0]
    BM = EXPERT_ROWS
    P = PACKED_SUBLANES
    rt3 = row_tok.reshape(NB, 1, BM)
    grid_spec = pltpu.PrefetchScalarGridSpec(
        num_scalar_prefetch=2,
        grid=(NB,),
        in_specs=[pl.BlockSpec((1, 1, BM), lambda i, be, nr: (i, 0, 0), memory_space=pltpu.SMEM),
                  pl.BlockSpec((BM, 1), lambda i, be, nr: (i, 0)),
                  pl.BlockSpec(memory_space=pl.ANY),
                  pl.BlockSpec((None, 1, D, H), lambda i, be, nr: (layer, be[i], 0, 0)),
                  pl.BlockSpec((None, 1, D, H), lambda i, be, nr: (layer, be[i], 0, 0)),
                  pl.BlockSpec((None, 1, H, D), lambda i, be, nr: (layer, be[i], 0, 0))],
        out_specs=pl.BlockSpec((BM * P, LANES), lambda i, be, nr: (i, 0)),
        scratch_shapes=[pltpu.VMEM(x_packed.shape, U32), pltpu.VMEM((BM * P, LANES), U32),
                        pltpu.VMEM((BM * 8, LANES), F32), pltpu.VMEM((D, H), BF16), pltpu.VMEM((D, H), BF16), pltpu.VMEM((H, D), BF16),
                        pltpu.SemaphoreType.DMA((1,))],
    )
    return pl.pallas_call(
        _experts_body,
        grid_spec=grid_spec,
        out_shape=jax.ShapeDtypeStruct((NB * BM * P, LANES), U32),
        compiler_params=pltpu.CompilerParams(dimension_semantics=("arbitrary",),
                                             vmem_limit_bytes=EXPERTS_VMEM_LIMIT_BYTES),
        name="experts",
    )(block_e, nreal, rt3, row_gate.reshape(NB * BM, 1), x_packed, w_gate, w_up, w_down)


def _row_gather_start(idx_of, n_rows, src_hbm, dst, sem):
    P = PACKED_SUBLANES

    def body(p, carry):
        for u in range(2):
            r = 2 * p + u
            pltpu.make_async_copy(src_hbm.at[pl.ds(pl.multiple_of(idx_of(r) * P, P), P), :],
                                  dst.at[pl.ds(pl.multiple_of(r * P, P), P), :], sem).start(priority=u)
        return carry
    lax.fori_loop(0, n_rows // 2, body, 0, unroll=4)


def _row_gather_wait(src_hbm, dst, sem):
    pltpu.make_async_copy(src_hbm.at[pl.ds(0, dst.shape[0]), :], dst, sem).wait()


def _combine_body(d_cur_ref, d_nxt_ref, x_ref, y_hbm, wsg_ref, wsu_ref, wsd_ref, lng_ref, lnb_ref,
                  o_ref, buf_ref, tiles_ref, sem):
    i = pl.program_id(0)
    nt = pl.num_programs(0)
    slot = i % 2
    tm = COMBINE_TILE

    def start(d_ref, s):
        _row_gather_start(lambda r: d_ref[0, 0, r], TOP_K * tm, y_hbm, buf_ref.at[s], sem.at[s])

    @pl.when(i == 0)
    def _():
        start(d_cur_ref, 0)

    @pl.when(i + 1 < nt)
    def _():
        start(d_nxt_ref, 1 - slot)

    x = x_ref[...]
    xb = x.astype(BF16)
    hs = jax.nn.silu(_dot(xb, wsg_ref[...])) * _dot(xb, wsu_ref[...])
    acc = _dot(hs.astype(BF16), wsd_ref[...])
    _row_gather_wait(y_hbm, buf_ref.at[slot], sem.at[slot])
    rows = tm * PACKED_SUBLANES
    mix = _unpack_tiles(buf_ref[slot, 0:rows, :])
    for k in range(1, TOP_K):
        mix = mix + _unpack_tiles(buf_ref[slot, k * rows:(k + 1) * rows, :])
    tiles_ref[...] = mix
    routed = _from_row_tiles(tiles_ref, tm)
    o_ref[...] = _standardize(DEEPNORM_ALPHA * x + (routed + acc)) * lng_ref[...] + lnb_ref[...]


def _combine_ln(x2d, y_tiles, dest_tiles, wsg, wsu, wsd, g_row, beta_row):
    T, D = x2d.shape
    H = wsg.shape[1]
    tm = COMBINE_TILE
    nt = T // tm
    row = pl.BlockSpec((1, D), lambda i: (0, 0))
    return pl.pallas_call(
        _combine_body,
        grid=(nt,),
        in_specs=[pl.BlockSpec((1, 1, TOP_K * tm), lambda i: (i, 0, 0), memory_space=pltpu.SMEM),
                  pl.BlockSpec((1, 1, TOP_K * tm), lambda i: (jnp.minimum(i + 1, nt - 1), 0, 0),
                               memory_space=pltpu.SMEM),
                  pl.BlockSpec((tm, D), lambda i: (i, 0)),
                  pl.BlockSpec(memory_space=pl.ANY),
                  pl.BlockSpec((D, H), lambda i: (0, 0)),
                  pl.BlockSpec((D, H), lambda i: (0, 0)),
                  pl.BlockSpec((H, D), lambda i: (0, 0)), row, row],
        out_specs=pl.BlockSpec((tm, D), lambda i: (i, 0)),
        out_shape=jax.ShapeDtypeStruct((T, D), F32),
        scratch_shapes=[pltpu.VMEM((2, TOP_K * tm * PACKED_SUBLANES, LANES), U32),
                        pltpu.VMEM((tm * 8, LANES), F32), pltpu.SemaphoreType.DMA((2,))],
        compiler_params=_cparams("arbitrary"),
        name="combine_ln",
    )(dest_tiles, dest_tiles, x2d, y_tiles, wsg, wsu, wsd, g_row, beta_row)


def _moe_ln(x2d, x_tiles, w_router, b_router, w_gate, w_up, w_down, ws_gate, ws_up, ws_down, g_row, beta_row,
            layer):
    T, D = x2d.shape
    E = w_router.shape[1]
    BM = EXPERT_ROWS
    NB = (T * TOP_K) // BM + E
    idx_t, gate_t, pos_t, cnt = _router(x2d, jnp.transpose(w_router), b_router.reshape(E, 1))
    counts = cnt[:, 0].astype(I32)
    padded = (counts + BM - 1) // BM * BM
    pend = jnp.cumsum(padded)
    pstart = pend - padded
    nreal = (pend[-1] // BM).astype(I32).reshape(1)
    block_start = jnp.arange(NB, dtype=I32) * BM
    block_e = jnp.minimum(jnp.sum((pend[None, :] <= block_start[:, None]).astype(I32), axis=1), E - 1)
    eids = jnp.arange(E, dtype=I32)
    dest_t = jnp.sum(jnp.where(idx_t[:, :, None] == eids, pstart, 0), axis=-1) + pos_t
    span = 2 * T
    tok = jnp.arange(T, dtype=I32)
    real_keys = (idx_t * span + tok[None, :]).reshape(-1)
    j = jnp.arange(BM, dtype=I32)
    pad_keys = jnp.where(j[None, :] < (padded - counts)[:, None], eids[:, None] * span + T + j[None, :],
                         jnp.iinfo(jnp.int32).max).reshape(-1)
    pad_gates = jnp.zeros((E * BM,), F32)
    keys, row_gate = lax.sort((jnp.concatenate([real_keys, pad_keys]),
                               jnp.concatenate([gate_t.reshape(-1), pad_gates])), num_keys=1)
    low = keys % span
    row_tok = jnp.where((keys < E * span) & (low < T), low, 0)
    y_tiles = _experts(x_tiles, row_tok, row_gate, block_e, nreal, w_gate, w_up, w_down, layer)
    nt = T // COMBINE_TILE
    dest_tiles = jnp.transpose(dest_t.reshape(TOP_K, nt, COMBINE_TILE), (1, 0, 2)).reshape(nt, 1, -1)
    return _combine_ln(x2d, y_tiles, dest_tiles, ws_gate.astype(BF16), ws_up.astype(BF16), ws_down.astype(BF16),
                       g_row, beta_row)


def _position_tables(S):
    pos = jnp.arange(S, dtype=F32)[:, None]
    attn_freq = ROPE_THETA ** (-jnp.arange(0, A_HEAD_DIM, 2, dtype=F32) / A_HEAD_DIM)
    ang = pos * attn_freq[None, :]
    cosa = jnp.tile(jnp.cos(ang), (1, 4))
    sina = jnp.tile(jnp.concatenate([-jnp.sin(ang), jnp.sin(ang)], axis=1), (1, 2))
    ret_freq = ROPE_THETA ** (-jnp.linspace(0.0, 1.0, R_KEY_DIM // 2, dtype=F32))
    angr = pos * ret_freq[None, :]
    cosr = jnp.tile(jnp.cos(angr), (1, 2))
    sinr = jnp.concatenate([-jnp.sin(angr), jnp.sin(angr)], axis=1)
    C = CHUNK
    log_g = jnp.log1p(-jnp.exp2(-5.0 - jnp.arange(R_HEADS, dtype=F32)))
    idx = jnp.arange(C, dtype=F32)
    diff = idx[:, None] - idx[None, :]
    decay = jnp.where(diff >= 0, jnp.exp(log_g[:, None, None] * jnp.maximum(diff, 0.0)), 0.0)
    kdec = jnp.exp(log_g[:, None] * (C - 1.0 - idx)[None, :])
    qdec = jnp.exp(log_g[:, None] * (idx + 1.0)[None, :])
    bc = lambda t: jnp.broadcast_to(t[:, :, None], (R_HEADS, C, R_KEY_DIM))
    gch = jnp.exp(log_g * C)
    return cosa, sina, cosr, sinr, decay, bc(qdec), bc(kdec), gch


def kernel(x, ab_w_in, ab_b_in, ab_sinks, ab_w_out, ab_b_out, cd_w_in, cd_b_in, sgu_w, sgu_b, conv_w, conv_b,
           lru_wa, lru_ba, lru_wx, lru_bx, lru_lambda, cd_w_out, cd_b_out, ln_mix_g, ln_mix_b, router_w, router_b,
           exp_w_gate, exp_w_up, exp_w_down, shared_w_gate, shared_w_up, shared_w_down, ln_ffn_g, ln_ffn_b):
    B, S, D = x.shape
    T = B * S
    tables = _position_tables(S)
    row = lambda v: v.reshape(1, -1)
    x2d = x.reshape(T, D)
    for layer in range(DEPTH):
        j = layer // 2
        if layer % 2 == 0:
            z = _inproj(x2d, ab_w_in[j].astype(BF16), row(ab_b_in[j]))
            y = _ab_core(z.reshape(B, S, -1), ab_sinks[j], tables)
            w_out, b_out = ab_w_out[j], ab_b_out[j]
        else:
            z = _inproj(x2d, cd_w_in[j].astype(BF16), row(cd_b_in[j]))
            y = _cd_core(z.reshape(B, S, -1), sgu_w[j], jnp.transpose(sgu_b[j]), conv_w[j], row(conv_b[j]),
                         lru_wa[j].astype(BF16), row(lru_ba[j]), lru_wx[j].astype(BF16), row(lru_bx[j]),
                         row(lru_lambda[j]))
            w_out, b_out = cd_w_out[j], cd_b_out[j]
        x2d, x_tiles = _outproj_ln(y.reshape(T, -1), w_out.astype(BF16), row(b_out), x2d,
                                   row(ln_mix_g[layer]), row(ln_mix_b[layer]))
        x2d = _moe_ln(x2d, x_tiles, router_w[layer], router_b[layer], exp_w_gate, exp_w_up, exp_w_down,
                      shared_w_gate[layer], shared_w_up[layer], shared_w_down[layer],
                      row(ln_ffn_g[layer]), row(ln_ffn_b[layer]), layer)
    return x2d.reshape(B, S, D)
```

```python
import functools
import math

import jax
import jax.numpy as jnp
from jax import lax
from jax.experimental import pallas as pl
from jax.experimental.pallas import tpu as pltpu

F32 = jnp.float32
BF16 = jnp.bfloat16
I32 = jnp.int32
U32 = jnp.uint32

CHUNK = 128
A_HEADS, A_KV_HEADS, A_HEAD_DIM = 16, 2, 64
R_HEADS, R_KEY_DIM, R_VAL_DIM = 4, 128, 256
SGU_GROUPS, SGU_GROUP_DIM = 4, 256
LRU_HEADS, LRU_BLOCK = 5, 256
CONV_WIDTH = 4
LRU_C = 8.0
TOP_K = 8
ROUTE_SCALE = 2.5
ROPE_THETA = 10000.0
LN_EPS = 1e-5
NEG_INF = -1e30
DEPTH = 4
DEEPNORM_ALPHA = (2 * DEPTH) ** 0.25

LANES = 128
PACKED_SUBLANES = 4
VMEM_LIMIT_BYTES = 48 * 1024 * 1024
EXPERTS_VMEM_LIMIT_BYTES = 56 * 1024 * 1024
ROW_TILE = 256
ROUTER_TILE = 512
EXPERT_ROWS = 512
COMBINE_TILE = 256


def _cparams(*sem):
    return pltpu.CompilerParams(dimension_semantics=sem, vmem_limit_bytes=VMEM_LIMIT_BYTES)


def _standardize(x):
    xc = x - jnp.mean(x, axis=-1, keepdims=True)
    return xc * lax.rsqrt(jnp.mean(xc * xc, axis=-1, keepdims=True) + LN_EPS)


def _dot(a, b):
    return jnp.dot(a, b, preferred_element_type=F32)


def _dot_nt(a, b):
    return lax.dot_general(a, b, (((1,), (1,)), ((), ())), preferred_element_type=F32)


def _inproj_body(x_ref, w_ref, b_ref, o_ref):
    o_ref[...] = _dot(x_ref[...].astype(BF16), w_ref[...]) + b_ref[...]


def _inproj(x2d, w_bf16, b_row):
    T, D = x2d.shape
    N = w_bf16.shape[1]
    return pl.pallas_call(
        _inproj_body,
        grid=(T // ROW_TILE,),
        in_specs=[pl.BlockSpec((ROW_TILE, D), lambda i: (i, 0)),
                  pl.BlockSpec((D, N), lambda i: (0, 0)),
                  pl.BlockSpec((1, N), lambda i: (0, 0))],
        out_specs=pl.BlockSpec((ROW_TILE, N), lambda i: (i, 0)),
        out_shape=jax.ShapeDtypeStruct((T, N), F32),
        compiler_params=_cparams("parallel"),
        name="inproj",
    )(x2d, w_bf16, b_row)


def _ab_body(sinks_ref, gch_ref, z_ref, cosa_ref, sina_ref, cosr_ref, sinr_ref, decay_ref, qdec_ref, kdec_ref,
             o_ref, kprev_ref, vprev_ref, state_ref):
    n = pl.program_id(1)
    C = CHUNK

    @pl.when(n == 0)
    def _():
        kprev_ref[...] = jnp.zeros_like(kprev_ref)
        vprev_ref[...] = jnp.zeros_like(vprev_ref)
        state_ref[...] = jnp.zeros_like(state_ref)

    lane = lax.broadcasted_iota(I32, (C, LANES), 1)
    lo_half = lane < 64
    first_32 = (lane & 32) == 0
    cosa = cosa_ref[...]
    sina = sina_ref[...]

    def rope64(x):
        swapped = jnp.where(first_32, pltpu.roll(x, 96, 1), pltpu.roll(x, 32, 1))
        return x * cosa + swapped * sina

    k = rope64(z_ref[0, :, 1024:1152])
    v = z_ref[0, :, 1152:1280]
    kw = jnp.concatenate([kprev_ref[...], k], axis=0)
    vw = jnp.concatenate([vprev_ref[...], v], axis=0)
    kprev_ref[...] = k
    vprev_ref[...] = v
    kw_sw = pltpu.roll(kw, 64, 1)
    vw_sw = pltpu.roll(vw, 64, 1)
    lane2 = lax.broadcasted_iota(I32, (2 * C, LANES), 1)
    lo2 = lane2 < 64

    rows = 8 * C
    qi = lax.broadcasted_iota(I32, (rows, 2 * C), 0) & (C - 1)
    kj = lax.broadcasted_iota(I32, (rows, 2 * C), 1)
    valid = (kj > qi) & (kj <= qi + C) & ((n > 0) | (kj >= C))
    rblk = lax.broadcasted_iota(I32, (rows, 1), 0) // C

    for g in range(A_KV_HEADS):
        if g == 0:
            kdup = jnp.where(lo2, kw, kw_sw)
            vlo = jnp.where(lo2, vw, 0.0)
            vhi = jnp.where(lo2, 0.0, vw_sw)
        else:
            kdup = jnp.where(lo2, kw_sw, kw)
            vlo = jnp.where(lo2, vw_sw, 0.0)
            vhi = jnp.where(lo2, 0.0, vw)
        qs = [rope64(z_ref[0, :, c * LANES:(c + 1) * LANES]) * (A_HEAD_DIM ** -0.5)
              for c in range(4 * g, 4 * g + 4)]
        qg = jnp.concatenate([jnp.where(lo_half, qc, 0.0) for qc in qs]
                             + [jnp.where(lo_half, 0.0, qc) for qc in qs], axis=0).astype(BF16)
        s = _dot_nt(qg, kdup.astype(BF16))
        s = jnp.where(valid, s, NEG_INF)
        sink = jnp.zeros((rows, 1), F32)
        for j in range(8):
            head = 8 * g + (2 * j if j < 4 else 2 * (j - 4) + 1)
            sink = jnp.where(rblk == j, sinks_ref[head], sink)
        m = jnp.maximum(jnp.max(s, axis=-1, keepdims=True), sink)
        e = jnp.exp(s - m)
        den = jnp.sum(e, axis=-1, keepdims=True) + jnp.exp(sink - m)
        p = (e * (1.0 / den)).astype(BF16)
        p2 = jnp.concatenate([p[:4 * C], p[4 * C:]], axis=1)
        vblk = jnp.concatenate([vlo, vhi], axis=0).astype(BF16)
        o = _dot(p2, vblk)
        for c in range(4):
            col = (4 * g + c) * LANES
            o_ref[0, :, col:col + LANES] = o[c * C:(c + 1) * C].astype(o_ref.dtype)

    cosr = cosr_ref[...]
    sinr = sinr_ref[...]

    def rope128(x):
        return x * cosr + pltpu.roll(x, 64, 1) * sinr

    q0, k0, v0, g0 = 1280, 1792, 2304, 3328
    for h in range(R_HEADS):
        qh = rope128(z_ref[0, :, q0 + h * 128:q0 + (h + 1) * 128])
        kh = rope128(z_ref[0, :, k0 + h * 128:k0 + (h + 1) * 128]) * (R_KEY_DIM ** -0.5)
        vb = z_ref[0, :, v0 + h * 256:v0 + (h + 1) * 256].astype(BF16)
        gh = z_ref[0, :, g0 + h * 256:g0 + (h + 1) * 256]
        s = _dot_nt(qh.astype(BF16), kh.astype(BF16)) * decay_ref[h]
        st = state_ref[h]
        o = _dot(s.astype(BF16), vb) + _dot((qh * qdec_ref[h]).astype(BF16), st.astype(BF16))
        kd_t = jnp.transpose(kh * kdec_ref[h]).astype(BF16)
        state_ref[h] = gch_ref[h] * st + _dot(kd_t, vb)
        y = jax.nn.silu(gh) * _standardize(o)
        col = 1024 + h * 256
        o_ref[0, :, col:col + 256] = y.astype(o_ref.dtype)


def _ab_core(z, sinks, tables):
    B, S, W = z.shape
    cosa, sina, cosr, sinr, decay, qdec, kdec, gch = tables
    smem = pl.BlockSpec(memory_space=pltpu.SMEM)
    tab = pl.BlockSpec((CHUNK, LANES), lambda b, n: (n, 0))
    const3 = pl.BlockSpec((R_HEADS, CHUNK, CHUNK), lambda b, n: (0, 0, 0))
    return pl.pallas_call(
        _ab_body,
        grid=(B, S // CHUNK),
        in_specs=[smem, smem,
                  pl.BlockSpec((1, CHUNK, W), lambda b, n: (b, n, 0)),
                  tab, tab, tab, tab, const3, const3, const3],
        out_specs=pl.BlockSpec((1, CHUNK, 2048), lambda b, n: (b, n, 0)),
        out_shape=jax.ShapeDtypeStruct((B, S, 2048), BF16),
        scratch_shapes=[pltpu.VMEM((CHUNK, LANES), F32), pltpu.VMEM((CHUNK, LANES), F32),
                        pltpu.VMEM((R_HEADS, R_KEY_DIM, R_VAL_DIM), F32)],
        compiler_params=_cparams("parallel", "arbitrary"),
        name="ab_core",
    )(sinks, gch, z, cosa, sina, cosr, sinr, decay, qdec, kdec)


def _cd_body(z_ref, sw_ref, sbt_ref, cw_ref, cb_ref, wa_ref, ba_ref, wx_ref, bx_ref, lam_ref,
             o_ref, xwin_ref, hc_ref):
    n = pl.program_id(1)
    C = CHUNK
    WL = LRU_HEADS * LRU_BLOCK

    @pl.when(n == 0)
    def _():
        xwin_ref[0:8, :] = jnp.zeros((8, WL), F32)
        hc_ref[...] = jnp.zeros_like(hc_ref)

    gu = jax.nn.gelu(z_ref[0, :, 0:1024])
    vn = _standardize(jax.nn.gelu(z_ref[0, :, 1024:2048])).astype(BF16)
    ri = lax.broadcasted_iota(I32, (C, C), 0)
    ci = lax.broadcasted_iota(I32, (C, C), 1)
    for g in range(SGU_GROUPS):
        w = jnp.where(ci <= ri, sw_ref[g], 0.0).astype(BF16)
        col = g * SGU_GROUP_DIM
        mixed = _dot(w, vn[:, col:col + SGU_GROUP_DIM]) + sbt_ref[:, g:g + 1]
        o_ref[0, :, col:col + SGU_GROUP_DIM] = (gu[:, col:col + SGU_GROUP_DIM] * mixed).astype(o_ref.dtype)

    xr = z_ref[0, :, 3328:3328 + WL]
    xwin_ref[8:8 + C, :] = xr
    xc = (cw_ref[3:4, :] * xr + cw_ref[2:3, :] * xwin_ref[7:7 + C, :] + cw_ref[1:2, :] * xwin_ref[6:6 + C, :]
          + cw_ref[0:1, :] * xwin_ref[5:5 + C, :] + cb_ref[...])
    xwin_ref[0:8, :] = xr[C - 8:C]
    xcb = xc.astype(BF16)
    ga, gi = [], []
    for h in range(LRU_HEADS):
        xb = xcb[:, h * LRU_BLOCK:(h + 1) * LRU_BLOCK]
        ga.append(_dot(xb, wa_ref[h]))
        gi.append(_dot(xb, wx_ref[h]))
    gate_r = jax.nn.sigmoid(jnp.concatenate(ga, axis=1) + ba_ref[...])
    gate_i = jax.nn.sigmoid(jnp.concatenate(gi, axis=1) + bx_ref[...])
    nl = -lam_ref[...]
    softplus = jnp.maximum(nl, 0.0) + jnp.log1p(jnp.exp(-jnp.abs(nl)))
    log_a = -LRU_C * gate_r * softplus
    a = jnp.exp(log_a)
    th = jnp.tanh(log_a)
    bv = jnp.sqrt(-2.0 * th / (1.0 - th)) * (gate_i * xc)
    r8 = lax.broadcasted_iota(I32, (C, WL), 0) & 7
    for d in (1, 2, 4):
        keep = r8 >= d
        a_sh = jnp.where(keep, pltpu.roll(a, d, 0), 1.0)
        b_sh = jnp.where(keep, pltpu.roll(bv, d, 0), 0.0)
        bv = a * b_sh + bv
        a = a * a_sh
    carry = hc_ref[...]
    hs = []
    for grp in range(C // 8):
        hg = bv[grp * 8:(grp + 1) * 8] + a[grp * 8:(grp + 1) * 8] * carry
        carry = hg[7:8]
        hs.append(hg)
    hc_ref[...] = carry
    hseq = jnp.concatenate(hs, axis=0)
    y = jax.nn.gelu(z_ref[0, :, 2048:2048 + WL]) * hseq
    o_ref[0, :, 1024:1024 + WL] = y.astype(o_ref.dtype)


def _cd_core(z, sgu_w, sgu_bt, conv_w, conv_b, wa, ba, wx, bx, lam):
    B, S, W = z.shape
    WL = LRU_HEADS * LRU_BLOCK
    full2 = lambda r, c: pl.BlockSpec((r, c), lambda b, n: (0, 0))
    full3 = lambda a, r, c: pl.BlockSpec((a, r, c), lambda b, n: (0, 0, 0))
    return pl.pallas_call(
        _cd_body,
        grid=(B, S // CHUNK),
        in_specs=[pl.BlockSpec((1, CHUNK, W), lambda b, n: (b, n, 0)),
                  full3(SGU_GROUPS, CHUNK, CHUNK), full2(CHUNK, SGU_GROUPS),
                  full2(CONV_WIDTH, WL), full2(1, WL),
                  full3(LRU_HEADS, LRU_BLOCK, LRU_BLOCK), full2(1, WL),
                  full3(LRU_HEADS, LRU_BLOCK, LRU_BLOCK), full2(1, WL), full2(1, WL)],
        out_specs=pl.BlockSpec((1, CHUNK, 1024 + WL), lambda b, n: (b, n, 0)),
        out_shape=jax.ShapeDtypeStruct((B, S, 1024 + WL), BF16),
        scratch_shapes=[pltpu.VMEM((8 + CHUNK, WL), F32), pltpu.VMEM((1, WL), F32)],
        compiler_params=_cparams("parallel", "arbitrary"),
        name="cd_core",
    )(z, sgu_w, sgu_bt, conv_w, conv_b, wa, ba, wx, bx, lam)


def _to_row_tiles(tiles_ref, val):
    n = val.shape[0]
    for j in range(val.shape[1] // LANES):
        tiles_ref[pl.ds(j, n, stride=8), :] = val[:, j * LANES:(j + 1) * LANES]


def _from_row_tiles(tiles_ref, n):
    return jnp.concatenate([tiles_ref[pl.ds(j, n, stride=8), :] for j in range(8)], axis=1)


def _pack_tiles(tiles):
    return pltpu.bitcast(tiles.astype(BF16), U32)


def _unpack_tiles(words):
    return pltpu.bitcast(words, BF16).astype(F32)


def _outproj_body(y_ref, w_ref, b_ref, x_ref, g_ref, beta_ref, o_ref, opacked_ref, tiles_ref):
    h = _dot(y_ref[...], w_ref[...]) + b_ref[...]
    xn = _standardize(DEEPNORM_ALPHA * x_ref[...] + h) * g_ref[...] + beta_ref[...]
    o_ref[...] = xn
    _to_row_tiles(tiles_ref, xn)
    opacked_ref[...] = _pack_tiles(tiles_ref[...])


def _outproj_ln(y2d, w_bf16, b_row, x2d, g_row, beta_row):
    T, Ky = y2d.shape
    D = x2d.shape[1]
    row = pl.BlockSpec((1, D), lambda i: (0, 0))
    return pl.pallas_call(
        _outproj_body,
        grid=(T // ROW_TILE,),
        in_specs=[pl.BlockSpec((ROW_TILE, Ky), lambda i: (i, 0)),
                  pl.BlockSpec((Ky, D), lambda i: (0, 0)), row,
                  pl.BlockSpec((ROW_TILE, D), lambda i: (i, 0)), row, row],
        out_specs=[pl.BlockSpec((ROW_TILE, D), lambda i: (i, 0)),
                   pl.BlockSpec((ROW_TILE * PACKED_SUBLANES, LANES), lambda i: (i, 0))],
        out_shape=[jax.ShapeDtypeStruct((T, D), F32), jax.ShapeDtypeStruct((T * PACKED_SUBLANES, LANES), U32)],
        scratch_shapes=[pltpu.VMEM((ROW_TILE * 8, LANES), F32)],
        compiler_params=_cparams("parallel"),
        name="outproj_ln",
    )(y2d, w_bf16, b_row, x2d, g_row, beta_row)


def _router_body(x_ref, wrt_ref, bcol_ref, idx_ref, gate_ref, pos_ref, cnt_ref, carry_ref):
    i = pl.program_id(0)
    E, tm = carry_ref.shape

    @pl.when(i == 0)
    def _():
        carry_ref[...] = jnp.zeros_like(carry_ref)

    logits = lax.dot_general(wrt_ref[...], x_ref[...], (((1,), (1,)), ((), ())),
                             precision=lax.Precision.HIGHEST, preferred_element_type=F32)
    scores = jax.nn.sigmoid(logits)
    sel = scores + bcol_ref[...]
    eio = lax.broadcasted_iota(I32, (E, tm), 0)
    onehots, idxs, tops = [], [], []
    for _ in range(TOP_K):
        m = jnp.max(sel, axis=0, keepdims=True)
        idx = jnp.min(jnp.where(sel == m, eio, E), axis=0, keepdims=True)
        hit = eio == idx
        tops.append(jnp.sum(jnp.where(hit, scores, 0.0), axis=0, keepdims=True))
        sel = jnp.where(hit, -jnp.inf, sel)
        onehots.append(hit)
        idxs.append(idx)
    chosen = onehots[0]
    for hit in onehots[1:]:
        chosen = chosen | hit
    mask = jnp.where(chosen, 1.0, 0.0).astype(BF16)
    ti = lax.broadcasted_iota(I32, (tm, tm), 0)
    tj = lax.broadcasted_iota(I32, (tm, tm), 1)
    earlier = jnp.where(ti < tj, 1.0, 0.0).astype(BF16)
    before = _dot(mask, earlier) + carry_ref[...]
    total = _dot(mask, jnp.ones((tm, tm), BF16))
    carry_ref[...] = carry_ref[...] + total
    cnt_ref[...] = carry_ref[:, 0:LANES]
    tsum = tops[0]
    for t in tops[1:]:
        tsum = tsum + t
    inv = ROUTE_SCALE / tsum
    gate_ref[...] = jnp.concatenate([t * inv for t in tops], axis=0)
    idx_ref[...] = jnp.concatenate(idxs, axis=0)
    pos_ref[...] = jnp.concatenate(
        [jnp.sum(jnp.where(hit, before, 0.0), axis=0, keepdims=True) for hit in onehots], axis=0).astype(I32)


def _router(x2d, wrt, bcol):
    T, D = x2d.shape
    E = wrt.shape[0]
    tm = ROUTER_TILE
    out8 = lambda dt: jax.ShapeDtypeStruct((TOP_K, T), dt)
    spec8 = pl.BlockSpec((TOP_K, tm), lambda i: (0, i))
    return pl.pallas_call(
        _router_body,
        grid=(T // tm,),
        in_specs=[pl.BlockSpec((tm, D), lambda i: (i, 0)),
                  pl.BlockSpec((E, D), lambda i: (0, 0)),
                  pl.BlockSpec((E, 1), lambda i: (0, 0))],
        out_specs=[spec8, spec8, spec8, pl.BlockSpec((E, LANES), lambda i: (0, 0))],
        out_shape=[out8(I32), out8(F32), out8(I32), jax.ShapeDtypeStruct((E, LANES), F32)],
        scratch_shapes=[pltpu.VMEM((E, tm), F32)],
        compiler_params=_cparams("arbitrary"),
        name="router",
    )(x2d, wrt, bcol)


def _experts_body(be_ref, nr_ref, rt_ref, rg_ref, x_hbm, wg_ref, wu_ref, wd_ref, y_ref,
                  xres_ref, stage_ref, tiles_ref, wgb_ref, wub_ref, wdb_ref, sem):
    i = pl.program_id(0)
    nreal = nr_ref[0]
    BM = EXPERT_ROWS
    P = PACKED_SUBLANES

    @pl.when(i == 0)
    def _():
        load_all = pltpu.make_async_copy(x_hbm, xres_ref, sem.at[0])
        load_all.start()
        load_all.wait()

    @pl.when(i < nreal)
    def _():
        @pl.when((i == 0) | (be_ref[i] != be_ref[jnp.maximum(i - 1, 0)]))
        def _():
            wgb_ref[...] = wg_ref[0].astype(BF16)
            wub_ref[...] = wu_ref[0].astype(BF16)
            wdb_ref[...] = wd_ref[0].astype(BF16)

        def copy_row(r, carry):
            t = rt_ref[0, 0, r]
            stage_ref[pl.ds(pl.multiple_of(r * P, P), P), :] = xres_ref[pl.ds(pl.multiple_of(t * P, P), P), :]
            return carry
        lax.fori_loop(0, BM, copy_row, 0, unroll=8)

        tiles_ref[...] = _unpack_tiles(stage_ref[...])
        xb = _from_row_tiles(tiles_ref, BM).astype(BF16)
        h = jax.nn.silu(_dot(xb, wgb_ref[...])) * _dot(xb, wub_ref[...])
        gate = jnp.transpose(jnp.broadcast_to(rg_ref[0], (LANES, BM)))
        y = _dot(h.astype(BF16), wdb_ref[...])
        for j in range(8):
            tiles_ref[pl.ds(j, BM, stride=8), :] = y[:, j * LANES:(j + 1) * LANES] * gate
        y_ref[...] = _pack_tiles(tiles_ref[...])

    @pl.when(i >= nreal)
    def _():
        y_ref[...] = _pack_tiles(jnp.zeros((BM * 8, LANES), F32))


def _experts(x_packed, row_tok, row_gate, block_e, nreal, w_gate, w_up, w_down, layer):
    _, E, D, H = w_gate.shape
    NB = block_e.shape[0]
    BM = EXPERT_ROWS
    P = PACKED_SUBLANES
    rt3 = row_tok.reshape(NB, 1, BM)
    grid_spec = pltpu.PrefetchScalarGridSpec(
        num_scalar_prefetch=2,
        grid=(NB,),
        in_specs=[pl.BlockSpec((1, 1, BM), lambda i, be, nr: (i, 0, 0), memory_space=pltpu.SMEM),
                  pl.BlockSpec((1, 1, BM), lambda i, be, nr: (i, 0, 0)),
                  pl.BlockSpec(memory_space=pl.ANY),
                  pl.BlockSpec((None, 1, D, H), lambda i, be, nr: (layer, be[i], 0, 0)),
                  pl.BlockSpec((None, 1, D, H), lambda i, be, nr: (layer, be[i], 0, 0)),
                  pl.BlockSpec((None, 1, H, D), lambda i, be, nr: (layer, be[i], 0, 0))],
        out_specs=pl.BlockSpec((BM * P, LANES), lambda i, be, nr: (i, 0)),
        scratch_shapes=[pltpu.VMEM(x_packed.shape, U32), pltpu.VMEM((BM * P, LANES), U32),
                        pltpu.VMEM((BM * 8, LANES), F32), pltpu.VMEM((D, H), BF16), pltpu.VMEM((D, H), BF16), pltpu.VMEM((H, D), BF16),
                        pltpu.SemaphoreType.DMA((1,))],
    )
    return pl.pallas_call(
        _experts_body,
        grid_spec=grid_spec,
        out_shape=jax.ShapeDtypeStruct((NB * BM * P, LANES), U32),
        compiler_params=pltpu.CompilerParams(dimension_semantics=("arbitrary",),
                                             vmem_limit_bytes=EXPERTS_VMEM_LIMIT_BYTES),
        name="experts",
    )(block_e, nreal, rt3, row_gate.reshape(NB, 1, BM), x_packed, w_gate, w_up, w_down)


def _row_gather_start(idx_of, n_rows, src_hbm, dst, sem):
    P = PACKED_SUBLANES

    def body(p, carry):
        for u in range(2):
            r = 2 * p + u
            pltpu.make_async_copy(src_hbm.at[pl.ds(pl.multiple_of(idx_of(r) * P, P), P), :],
                                  dst.at[pl.ds(pl.multiple_of(r * P, P), P), :], sem).start(priority=u)
        return carry
    lax.fori_loop(0, n_rows // 2, body, 0, unroll=4)


def _row_gather_wait(src_hbm, dst, sem):
    pltpu.make_async_copy(src_hbm.at[pl.ds(0, dst.shape[0]), :], dst, sem).wait()


def _combine_body(d_cur_ref, d_nxt_ref, x_ref, y_hbm, wsg_ref, wsu_ref, wsd_ref, lng_ref, lnb_ref,
                  o_ref, buf_ref, tiles_ref, sem):
    i = pl.program_id(0)
    nt = pl.num_programs(0)
    slot = i % 2
    tm = COMBINE_TILE

    def start(d_ref, s):
        _row_gather_start(lambda r: d_ref[0, 0, r], TOP_K * tm, y_hbm, buf_ref.at[s], sem.at[s])

    @pl.when(i == 0)
    def _():
        start(d_cur_ref, 0)

    @pl.when(i + 1 < nt)
    def _():
        start(d_nxt_ref, 1 - slot)

    x = x_ref[...]
    xb = x.astype(BF16)
    hs = jax.nn.silu(_dot(xb, wsg_ref[...])) * _dot(xb, wsu_ref[...])
    acc = _dot(hs.astype(BF16), wsd_ref[...])
    _row_gather_wait(y_hbm, buf_ref.at[slot], sem.at[slot])
    rows = tm * PACKED_SUBLANES
    mix = _unpack_tiles(buf_ref[slot, 0:rows, :])
    for k in range(1, TOP_K):
        mix = mix + _unpack_tiles(buf_ref[slot, k * rows:(k + 1) * rows, :])
    tiles_ref[...] = mix
    routed = _from_row_tiles(tiles_ref, tm)
    o_ref[...] = _standardize(DEEPNORM_ALPHA * x + (routed + acc)) * lng_ref[...] + lnb_ref[...]


def _combine_ln(x2d, y_tiles, dest_tiles, wsg, wsu, wsd, g_row, beta_row):
    T, D = x2d.shape
    H = wsg.shape[1]
    tm = COMBINE_TILE
    nt = T // tm
    row = pl.BlockSpec((1, D), lambda i: (0, 0))
    return pl.pallas_call(
        _combine_body,
        grid=(nt,),
        in_specs=[pl.BlockSpec((1, 1, TOP_K * tm), lambda i: (i, 0, 0), memory_space=pltpu.SMEM),
                  pl.BlockSpec((1, 1, TOP_K * tm), lambda i: (jnp.minimum(i + 1, nt - 1), 0, 0),
                               memory_space=pltpu.SMEM),
                  pl.BlockSpec((tm, D), lambda i: (i, 0)),
                  pl.BlockSpec(memory_space=pl.ANY),
                  pl.BlockSpec((D, H), lambda i: (0, 0)),
                  pl.BlockSpec((D, H), lambda i: (0, 0)),
                  pl.BlockSpec((H, D), lambda i: (0, 0)), row, row],
        out_specs=pl.BlockSpec((tm, D), lambda i: (i, 0)),
        out_shape=jax.ShapeDtypeStruct((T, D), F32),
        scratch_shapes=[pltpu.VMEM((2, TOP_K * tm * PACKED_SUBLANES, LANES), U32),
                        pltpu.VMEM((tm * 8, LANES), F32), pltpu.SemaphoreType.DMA((2,))],
        compiler_params=_cparams("arbitrary"),
        name="combine_ln",
    )(dest_tiles, dest_tiles, x2d, y_tiles, wsg, wsu, wsd, g_row, beta_row)


def _moe_ln(x2d, x_tiles, w_router, b_router, w_gate, w_up, w_down, ws_gate, ws_up, ws_down, g_row, beta_row,
            layer):
    T, D = x2d.shape
    E = w_router.shape[1]
    BM = EXPERT_ROWS
    NB = (T * TOP_K) // BM + E
    idx_t, gate_t, pos_t, cnt = _router(x2d, jnp.transpose(w_router), b_router.reshape(E, 1))
    counts = cnt[:, 0].astype(I32)
    padded = (counts + BM - 1) // BM * BM
    pend = jnp.cumsum(padded)
    pstart = pend - padded
    nreal = (pend[-1] // BM).astype(I32).reshape(1)
    block_start = jnp.arange(NB, dtype=I32) * BM
    block_e = jnp.minimum(jnp.sum((pend[None, :] <= block_start[:, None]).astype(I32), axis=1), E - 1)
    eids = jnp.arange(E, dtype=I32)
    dest_t = jnp.sum(jnp.where(idx_t[:, :, None] == eids, pstart, 0), axis=-1) + pos_t
    span = 2 * T
    tok = jnp.arange(T, dtype=I32)
    real_keys = (idx_t * span + tok[None, :]).reshape(-1)
    j = jnp.arange(BM, dtype=I32)
    pad_keys = jnp.where(j[None, :] < (padded - counts)[:, None], eids[:, None] * span + T + j[None, :],
                         jnp.iinfo(jnp.int32).max).reshape(-1)
    pad_gates = jnp.zeros((E * BM,), F32)
    keys, row_gate = lax.sort((jnp.concatenate([real_keys, pad_keys]),
                               jnp.concatenate([gate_t.reshape(-1), pad_gates])), num_keys=1)
    low = keys % span
    row_tok = jnp.where((keys < E * span) & (low < T), low, 0)
    y_tiles = _experts(x_tiles, row_tok, row_gate, block_e, nreal, w_gate, w_up, w_down, layer)
    nt = T // COMBINE_TILE
    dest_tiles = jnp.transpose(dest_t.reshape(TOP_K, nt, COMBINE_TILE), (1, 0, 2)).reshape(nt, 1, -1)
    return _combine_ln(x2d, y_tiles, dest_tiles, ws_gate.astype(BF16), ws_up.astype(BF16), ws_down.astype(BF16),
                       g_row, beta_row)


def _position_tables(S):
    pos = jnp.arange(S, dtype=F32)[:, None]
    attn_freq = ROPE_THETA ** (-jnp.arange(0, A_HEAD_DIM, 2, dtype=F32) / A_HEAD_DIM)
    ang = pos * attn_freq[None, :]
    cosa = jnp.tile(jnp.cos(ang), (1, 4))
    sina = jnp.tile(jnp.concatenate([-jnp.sin(ang), jnp.sin(ang)], axis=1), (1, 2))
    ret_freq = ROPE_THETA ** (-jnp.linspace(0.0, 1.0, R_KEY_DIM // 2, dtype=F32))
    angr = pos * ret_freq[None, :]
    cosr = jnp.tile(jnp.cos(angr), (1, 2))
    sinr = jnp.concatenate([-jnp.sin(angr), jnp.sin(angr)], axis=1)
    C = CHUNK
    log_g = jnp.log1p(-jnp.exp2(-5.0 - jnp.arange(R_HEADS, dtype=F32)))
    idx = jnp.arange(C, dtype=F32)
    diff = idx[:, None] - idx[None, :]
    decay = jnp.where(diff >= 0, jnp.exp(log_g[:, None, None] * jnp.maximum(diff, 0.0)), 0.0)
    kdec = jnp.exp(log_g[:, None] * (C - 1.0 - idx)[None, :])
    qdec = jnp.exp(log_g[:, None] * (idx + 1.0)[None, :])
    bc = lambda t: jnp.broadcast_to(t[:, :, None], (R_HEADS, C, R_KEY_DIM))
    gch = jnp.exp(log_g * C)
    return cosa, sina, cosr, sinr, decay, bc(qdec), bc(kdec), gch


def kernel(x, ab_w_in, ab_b_in, ab_sinks, ab_w_out, ab_b_out, cd_w_in, cd_b_in, sgu_w, sgu_b, conv_w, conv_b,
           lru_wa, lru_ba, lru_wx, lru_bx, lru_lambda, cd_w_out, cd_b_out, ln_mix_g, ln_mix_b, router_w, router_b,
           exp_w_gate, exp_w_up, exp_w_down, shared_w_gate, shared_w_up, shared_w_down, ln_ffn_g, ln_ffn_b):
    B, S, D = x.shape
    T = B * S
    tables = _position_tables(S)
    row = lambda v: v.reshape(1, -1)
    x2d = x.reshape(T, D)
    for layer in range(DEPTH):
        j = layer // 2
        if layer % 2 == 0:
            z = _inproj(x2d, ab_w_in[j].astype(BF16), row(ab_b_in[j]))
            y = _ab_core(z.reshape(B, S, -1), ab_sinks[j], tables)
            w_out, b_out = ab_w_out[j], ab_b_out[j]
        else:
            z = _inproj(x2d, cd_w_in[j].astype(BF16), row(cd_b_in[j]))
            y = _cd_core(z.reshape(B, S, -1), sgu_w[j], jnp.transpose(sgu_b[j]), conv_w[j], row(conv_b[j]),
                         lru_wa[j].astype(BF16), row(lru_ba[j]), lru_wx[j].astype(BF16), row(lru_bx[j]),
                         row(lru_lambda[j]))
            w_out, b_out = cd_w_out[j], cd_b_out[j]
        x2d, x_tiles = _outproj_ln(y.reshape(T, -1), w_out.astype(BF16), row(b_out), x2d,
                                   row(ln_mix_g[layer]), row(ln_mix_b[layer]))
        x2d = _moe_ln(x2d, x_tiles, router_w[layer], router_b[layer], exp_w_gate, exp_w_up, exp_w_down,
                      shared_w_gate[layer], shared_w_up[layer], shared_w_down[layer],
                      row(ln_ffn_g[layer]), row(ln_ffn_b[layer]), layer)
    return x2d.reshape(B, S, D)
```

```python
import functools
import math

import jax
import jax.numpy as jnp
from jax import lax
from jax.experimental import pallas as pl
from jax.experimental.pallas import tpu as pltpu

F32 = jnp.float32
BF16 = jnp.bfloat16
I32 = jnp.int32
U32 = jnp.uint32

CHUNK = 128
A_HEADS, A_KV_HEADS, A_HEAD_DIM = 16, 2, 64
R_HEADS, R_KEY_DIM, R_VAL_DIM = 4, 128, 256
SGU_GROUPS, SGU_GROUP_DIM = 4, 256
LRU_HEADS, LRU_BLOCK = 5, 256
CONV_WIDTH = 4
LRU_C = 8.0
TOP_K = 8
ROUTE_SCALE = 2.5
ROPE_THETA = 10000.0
LN_EPS = 1e-5
NEG_INF = -1e30
DEPTH = 4
DEEPNORM_ALPHA = (2 * DEPTH) ** 0.25

LANES = 128
PACKED_SUBLANES = 4
VMEM_LIMIT_BYTES = 48 * 1024 * 1024
EXPERTS_VMEM_LIMIT_BYTES = 56 * 1024 * 1024
ROW_TILE = 256
ROUTER_TILE = 512
EXPERT_ROWS = 512
COMBINE_TILE = 256
GATHER_UNROLL = 16


def _cparams(*sem):
    return pltpu.CompilerParams(dimension_semantics=sem, vmem_limit_bytes=VMEM_LIMIT_BYTES)


def _standardize(x):
    xc = x - jnp.mean(x, axis=-1, keepdims=True)
    return xc * lax.rsqrt(jnp.mean(xc * xc, axis=-1, keepdims=True) + LN_EPS)


def _dot(a, b):
    return jnp.dot(a, b, preferred_element_type=F32)


def _dot_nt(a, b):
    return lax.dot_general(a, b, (((1,), (1,)), ((), ())), preferred_element_type=F32)


def _inproj_body(x_ref, w_ref, b_ref, o_ref):
    o_ref[...] = _dot(x_ref[...].astype(BF16), w_ref[...]) + b_ref[...]


def _inproj(x2d, w_bf16, b_row):
    T, D = x2d.shape
    N = w_bf16.shape[1]
    return pl.pallas_call(
        _inproj_body,
        grid=(T // ROW_TILE,),
        in_specs=[pl.BlockSpec((ROW_TILE, D), lambda i: (i, 0)),
                  pl.BlockSpec((D, N), lambda i: (0, 0)),
                  pl.BlockSpec((1, N), lambda i: (0, 0))],
        out_specs=pl.BlockSpec((ROW_TILE, N), lambda i: (i, 0)),
        out_shape=jax.ShapeDtypeStruct((T, N), F32),
        compiler_params=_cparams("parallel"),
        name="inproj",
    )(x2d, w_bf16, b_row)


def _ab_body(sinks_ref, gch_ref, z_ref, cosa_ref, sina_ref, cosr_ref, sinr_ref, decay_ref, qdec_ref, kdec_ref,
             o_ref, kprev_ref, vprev_ref, state_ref):
    n = pl.program_id(1)
    C = CHUNK

    @pl.when(n == 0)
    def _():
        kprev_ref[...] = jnp.zeros_like(kprev_ref)
        vprev_ref[...] = jnp.zeros_like(vprev_ref)
        state_ref[...] = jnp.zeros_like(state_ref)

    lane = lax.broadcasted_iota(I32, (C, LANES), 1)
    lo_half = lane < 64
    first_32 = (lane & 32) == 0
    cosa = cosa_ref[...]
    sina = sina_ref[...]

    def rope64(x):
        swapped = jnp.where(first_32, pltpu.roll(x, 96, 1), pltpu.roll(x, 32, 1))
        return x * cosa + swapped * sina

    k = rope64(z_ref[0, :, 1024:1152])
    v = z_ref[0, :, 1152:1280]
    kw = jnp.concatenate([kprev_ref[...], k], axis=0)
    vw = jnp.concatenate([vprev_ref[...], v], axis=0)
    kprev_ref[...] = k
    vprev_ref[...] = v
    kw_sw = pltpu.roll(kw, 64, 1)
    vw_sw = pltpu.roll(vw, 64, 1)
    lane2 = lax.broadcasted_iota(I32, (2 * C, LANES), 1)
    lo2 = lane2 < 64

    rows = 8 * C
    qi = lax.broadcasted_iota(I32, (rows, 2 * C), 0) & (C - 1)
    kj = lax.broadcasted_iota(I32, (rows, 2 * C), 1)
    valid = (kj > qi) & (kj <= qi + C) & ((n > 0) | (kj >= C))
    rblk = lax.broadcasted_iota(I32, (rows, 1), 0) // C

    for g in range(A_KV_HEADS):
        if g == 0:
            kdup = jnp.where(lo2, kw, kw_sw)
            vlo = jnp.where(lo2, vw, 0.0)
            vhi = jnp.where(lo2, 0.0, vw_sw)
        else:
            kdup = jnp.where(lo2, kw_sw, kw)
            vlo = jnp.where(lo2, vw_sw, 0.0)
            vhi = jnp.where(lo2, 0.0, vw)
        qs = [rope64(z_ref[0, :, c * LANES:(c + 1) * LANES]) * (A_HEAD_DIM ** -0.5)
              for c in range(4 * g, 4 * g + 4)]
        qg = jnp.concatenate([jnp.where(lo_half, qc, 0.0) for qc in qs]
                             + [jnp.where(lo_half, 0.0, qc) for qc in qs], axis=0).astype(BF16)
        s = _dot_nt(qg, kdup.astype(BF16))
        s = jnp.where(valid, s, NEG_INF)
        sink = jnp.zeros((rows, 1), F32)
        for j in range(8):
            head = 8 * g + (2 * j if j < 4 else 2 * (j - 4) + 1)
            sink = jnp.where(rblk == j, sinks_ref[head], sink)
        m = jnp.maximum(jnp.max(s, axis=-1, keepdims=True), sink)
        e = jnp.exp(s - m)
        den = jnp.sum(e, axis=-1, keepdims=True) + jnp.exp(sink - m)
        p = (e * (1.0 / den)).astype(BF16)
        p2 = jnp.concatenate([p[:4 * C], p[4 * C:]], axis=1)
        vblk = jnp.concatenate([vlo, vhi], axis=0).astype(BF16)
        o = _dot(p2, vblk)
        for c in range(4):
            col = (4 * g + c) * LANES
            o_ref[0, :, col:col + LANES] = o[c * C:(c + 1) * C].astype(o_ref.dtype)

    cosr = cosr_ref[...]
    sinr = sinr_ref[...]

    def rope128(x):
        return x * cosr + pltpu.roll(x, 64, 1) * sinr

    q0, k0, v0, g0 = 1280, 1792, 2304, 3328
    for h in range(R_HEADS):
        qh = rope128(z_ref[0, :, q0 + h * 128:q0 + (h + 1) * 128])
        kh = rope128(z_ref[0, :, k0 + h * 128:k0 + (h + 1) * 128]) * (R_KEY_DIM ** -0.5)
        vb = z_ref[0, :, v0 + h * 256:v0 + (h + 1) * 256].astype(BF16)
        gh = z_ref[0, :, g0 + h * 256:g0 + (h + 1) * 256]
        s = _dot_nt(qh.astype(BF16), kh.astype(BF16)) * decay_ref[h]
        st = state_ref[h]
        o = _dot(s.astype(BF16), vb) + _dot((qh * qdec_ref[h]).astype(BF16), st.astype(BF16))
        kd_t = jnp.transpose(kh * kdec_ref[h]).astype(BF16)
        state_ref[h] = gch_ref[h] * st + _dot(kd_t, vb)
        y = jax.nn.silu(gh) * _standardize(o)
        col = 1024 + h * 256
        o_ref[0, :, col:col + 256] = y.astype(o_ref.dtype)


def _ab_core(z, sinks, tables):
    B, S, W = z.shape
    cosa, sina, cosr, sinr, decay, qdec, kdec, gch = tables
    smem = pl.BlockSpec(memory_space=pltpu.SMEM)
    tab = pl.BlockSpec((CHUNK, LANES), lambda b, n: (n, 0))
    const3 = pl.BlockSpec((R_HEADS, CHUNK, CHUNK), lambda b, n: (0, 0, 0))
    return pl.pallas_call(
        _ab_body,
        grid=(B, S // CHUNK),
        in_specs=[smem, smem,
                  pl.BlockSpec((1, CHUNK, W), lambda b, n: (b, n, 0)),
                  tab, tab, tab, tab, const3, const3, const3],
        out_specs=pl.BlockSpec((1, CHUNK, 2048), lambda b, n: (b, n, 0)),
        out_shape=jax.ShapeDtypeStruct((B, S, 2048), BF16),
        scratch_shapes=[pltpu.VMEM((CHUNK, LANES), F32), pltpu.VMEM((CHUNK, LANES), F32),
                        pltpu.VMEM((R_HEADS, R_KEY_DIM, R_VAL_DIM), F32)],
        compiler_params=_cparams("parallel", "arbitrary"),
        name="ab_core",
    )(sinks, gch, z, cosa, sina, cosr, sinr, decay, qdec, kdec)


def _cd_body(z_ref, sw_ref, sbt_ref, cw_ref, cb_ref, wa_ref, ba_ref, wx_ref, bx_ref, lam_ref,
             o_ref, xwin_ref, hc_ref):
    n = pl.program_id(1)
    C = CHUNK
    WL = LRU_HEADS * LRU_BLOCK

    @pl.when(n == 0)
    def _():
        xwin_ref[0:8, :] = jnp.zeros((8, WL), F32)
        hc_ref[...] = jnp.zeros_like(hc_ref)

    gu = jax.nn.gelu(z_ref[0, :, 0:1024])
    vn = _standardize(jax.nn.gelu(z_ref[0, :, 1024:2048])).astype(BF16)
    ri = lax.broadcasted_iota(I32, (C, C), 0)
    ci = lax.broadcasted_iota(I32, (C, C), 1)
    for g in range(SGU_GROUPS):
        w = jnp.where(ci <= ri, sw_ref[g], 0.0).astype(BF16)
        col = g * SGU_GROUP_DIM
        mixed = _dot(w, vn[:, col:col + SGU_GROUP_DIM]) + sbt_ref[:, g:g + 1]
        o_ref[0, :, col:col + SGU_GROUP_DIM] = (gu[:, col:col + SGU_GROUP_DIM] * mixed).astype(o_ref.dtype)

    xr = z_ref[0, :, 3328:3328 + WL]
    xwin_ref[8:8 + C, :] = xr
    xc = (cw_ref[3:4, :] * xr + cw_ref[2:3, :] * xwin_ref[7:7 + C, :] + cw_ref[1:2, :] * xwin_ref[6:6 + C, :]
          + cw_ref[0:1, :] * xwin_ref[5:5 + C, :] + cb_ref[...])
    xwin_ref[0:8, :] = xr[C - 8:C]
    xcb = xc.astype(BF16)
    ga, gi = [], []
    for h in range(LRU_HEADS):
        xb = xcb[:, h * LRU_BLOCK:(h + 1) * LRU_BLOCK]
        ga.append(_dot(xb, wa_ref[h]))
        gi.append(_dot(xb, wx_ref[h]))
    gate_r = jax.nn.sigmoid(jnp.concatenate(ga, axis=1) + ba_ref[...])
    gate_i = jax.nn.sigmoid(jnp.concatenate(gi, axis=1) + bx_ref[...])
    nl = -lam_ref[...]
    softplus = jnp.maximum(nl, 0.0) + jnp.log1p(jnp.exp(-jnp.abs(nl)))
    log_a = -LRU_C * gate_r * softplus
    a = jnp.exp(log_a)
    th = jnp.tanh(log_a)
    bv = jnp.sqrt(-2.0 * th / (1.0 - th)) * (gate_i * xc)
    r8 = lax.broadcasted_iota(I32, (C, WL), 0) & 7
    for d in (1, 2, 4):
        keep = r8 >= d
        a_sh = jnp.where(keep, pltpu.roll(a, d, 0), 1.0)
        b_sh = jnp.where(keep, pltpu.roll(bv, d, 0), 0.0)
        bv = a * b_sh + bv
        a = a * a_sh
    carry = hc_ref[...]
    hs = []
    for grp in range(C // 8):
        hg = bv[grp * 8:(grp + 1) * 8] + a[grp * 8:(grp + 1) * 8] * carry
        carry = hg[7:8]
        hs.append(hg)
    hc_ref[...] = carry
    hseq = jnp.concatenate(hs, axis=0)
    y = jax.nn.gelu(z_ref[0, :, 2048:2048 + WL]) * hseq
    o_ref[0, :, 1024:1024 + WL] = y.astype(o_ref.dtype)


def _cd_core(z, sgu_w, sgu_bt, conv_w, conv_b, wa, ba, wx, bx, lam):
    B, S, W = z.shape
    WL = LRU_HEADS * LRU_BLOCK
    full2 = lambda r, c: pl.BlockSpec((r, c), lambda b, n: (0, 0))
    full3 = lambda a, r, c: pl.BlockSpec((a, r, c), lambda b, n: (0, 0, 0))
    return pl.pallas_call(
        _cd_body,
        grid=(B, S // CHUNK),
        in_specs=[pl.BlockSpec((1, CHUNK, W), lambda b, n: (b, n, 0)),
                  full3(SGU_GROUPS, CHUNK, CHUNK), full2(CHUNK, SGU_GROUPS),
                  full2(CONV_WIDTH, WL), full2(1, WL),
                  full3(LRU_HEADS, LRU_BLOCK, LRU_BLOCK), full2(1, WL),
                  full3(LRU_HEADS, LRU_BLOCK, LRU_BLOCK), full2(1, WL), full2(1, WL)],
        out_specs=pl.BlockSpec((1, CHUNK, 1024 + WL), lambda b, n: (b, n, 0)),
        out_shape=jax.ShapeDtypeStruct((B, S, 1024 + WL), BF16),
        scratch_shapes=[pltpu.VMEM((8 + CHUNK, WL), F32), pltpu.VMEM((1, WL), F32)],
        compiler_params=_cparams("parallel", "arbitrary"),
        name="cd_core",
    )(z, sgu_w, sgu_bt, conv_w, conv_b, wa, ba, wx, bx, lam)


def _to_row_tiles(tiles_ref, val):
    n = val.shape[0]
    for j in range(val.shape[1] // LANES):
        tiles_ref[pl.ds(j, n, stride=8), :] = val[:, j * LANES:(j + 1) * LANES]


def _from_row_tiles(tiles_ref, n):
    return jnp.concatenate([tiles_ref[pl.ds(j, n, stride=8), :] for j in range(8)], axis=1)


def _pack_tiles(tiles):
    return pltpu.bitcast(tiles.astype(BF16), U32)


def _unpack_tiles(words):
    return pltpu.bitcast(words, BF16).astype(F32)


def _outproj_body(y_ref, w_ref, b_ref, x_ref, g_ref, beta_ref, o_ref, opacked_ref, tiles_ref):
    h = _dot(y_ref[...], w_ref[...]) + b_ref[...]
    xn = _standardize(DEEPNORM_ALPHA * x_ref[...] + h) * g_ref[...] + beta_ref[...]
    o_ref[...] = xn
    _to_row_tiles(tiles_ref, xn)
    opacked_ref[...] = _pack_tiles(tiles_ref[...])


def _outproj_ln(y2d, w_bf16, b_row, x2d, g_row, beta_row):
    T, Ky = y2d.shape
    D = x2d.shape[1]
    row = pl.BlockSpec((1, D), lambda i: (0, 0))
    return pl.pallas_call(
        _outproj_body,
        grid=(T // ROW_TILE,),
        in_specs=[pl.BlockSpec((ROW_TILE, Ky), lambda i: (i, 0)),
                  pl.BlockSpec((Ky, D), lambda i: (0, 0)), row,
                  pl.BlockSpec((ROW_TILE, D), lambda i: (i, 0)), row, row],
        out_specs=[pl.BlockSpec((ROW_TILE, D), lambda i: (i, 0)),
                   pl.BlockSpec((ROW_TILE * PACKED_SUBLANES, LANES), lambda i: (i, 0))],
        out_shape=[jax.ShapeDtypeStruct((T, D), F32), jax.ShapeDtypeStruct((T * PACKED_SUBLANES, LANES), U32)],
        scratch_shapes=[pltpu.VMEM((ROW_TILE * 8, LANES), F32)],
        compiler_params=_cparams("parallel"),
        name="outproj_ln",
    )(y2d, w_bf16, b_row, x2d, g_row, beta_row)


def _router_body(x_ref, wrt_ref, bcol_ref, idx_ref, gate_ref, pos_ref, cnt_ref, carry_ref):
    i = pl.program_id(0)
    E, tm = carry_ref.shape

    @pl.when(i == 0)
    def _():
        carry_ref[...] = jnp.zeros_like(carry_ref)

    logits = lax.dot_general(wrt_ref[...], x_ref[...], (((1,), (1,)), ((), ())),
                             precision=lax.Precision.HIGHEST, preferred_element_type=F32)
    scores = jax.nn.sigmoid(logits)
    sel = scores + bcol_ref[...]
    eio = lax.broadcasted_iota(I32, (E, tm), 0)
    onehots, idxs, tops = [], [], []
    for _ in range(TOP_K):
        m = jnp.max(sel, axis=0, keepdims=True)
        idx = jnp.min(jnp.where(sel == m, eio, E), axis=0, keepdims=True)
        hit = eio == idx
        tops.append(jnp.sum(jnp.where(hit, scores, 0.0), axis=0, keepdims=True))
        sel = jnp.where(hit, -jnp.inf, sel)
        onehots.append(hit)
        idxs.append(idx)
    chosen = onehots[0]
    for hit in onehots[1:]:
        chosen = chosen | hit
    mask = jnp.where(chosen, 1.0, 0.0).astype(BF16)
    ti = lax.broadcasted_iota(I32, (tm, tm), 0)
    tj = lax.broadcasted_iota(I32, (tm, tm), 1)
    earlier = jnp.where(ti < tj, 1.0, 0.0).astype(BF16)
    before = _dot(mask, earlier) + carry_ref[...]
    total = _dot(mask, jnp.ones((tm, tm), BF16))
    carry_ref[...] = carry_ref[...] + total
    cnt_ref[...] = carry_ref[:, 0:LANES]
    tsum = tops[0]
    for t in tops[1:]:
        tsum = tsum + t
    inv = ROUTE_SCALE / tsum
    gate_ref[...] = jnp.concatenate([t * inv for t in tops], axis=0)
    idx_ref[...] = jnp.concatenate(idxs, axis=0)
    pos_ref[...] = jnp.concatenate(
        [jnp.sum(jnp.where(hit, before, 0.0), axis=0, keepdims=True) for hit in onehots], axis=0).astype(I32)


def _router(x2d, wrt, bcol):
    T, D = x2d.shape
    E = wrt.shape[0]
    tm = ROUTER_TILE
    out8 = lambda dt: jax.ShapeDtypeStruct((TOP_K, T), dt)
    spec8 = pl.BlockSpec((TOP_K, tm), lambda i: (0, i))
    return pl.pallas_call(
        _router_body,
        grid=(T // tm,),
        in_specs=[pl.BlockSpec((tm, D), lambda i: (i, 0)),
                  pl.BlockSpec((E, D), lambda i: (0, 0)),
                  pl.BlockSpec((E, 1), lambda i: (0, 0))],
        out_specs=[spec8, spec8, spec8, pl.BlockSpec((E, LANES), lambda i: (0, 0))],
        out_shape=[out8(I32), out8(F32), out8(I32), jax.ShapeDtypeStruct((E, LANES), F32)],
        scratch_shapes=[pltpu.VMEM((E, tm), F32)],
        compiler_params=_cparams("arbitrary"),
        name="router",
    )(x2d, wrt, bcol)


def _experts_body(be_ref, nr_ref, rt_ref, rg_ref, x_hbm, wg_ref, wu_ref, wd_ref, y_ref,
                  xres_ref, stage_ref, tiles_ref, wgb_ref, wub_ref, wdb_ref, sem):
    i = pl.program_id(0)
    nreal = nr_ref[0]
    BM = EXPERT_ROWS
    P = PACKED_SUBLANES

    @pl.when(i == 0)
    def _():
        load_all = pltpu.make_async_copy(x_hbm, xres_ref, sem.at[0])
        load_all.start()
        load_all.wait()

    @pl.when(i < nreal)
    def _():
        @pl.when((i == 0) | (be_ref[i] != be_ref[jnp.maximum(i - 1, 0)]))
        def _():
            wgb_ref[...] = wg_ref[0].astype(BF16)
            wub_ref[...] = wu_ref[0].astype(BF16)
            wdb_ref[...] = wd_ref[0].astype(BF16)

        def copy_row(r, carry):
            src = pl.multiple_of(rt_ref[0, 0, r], P)
            stage_ref[pl.ds(pl.multiple_of(r * P, P), P), :] = xres_ref[pl.ds(src, P), :]
            return carry
        lax.fori_loop(0, BM, copy_row, 0, unroll=GATHER_UNROLL)

        tiles_ref[...] = _unpack_tiles(stage_ref[...])
        xb = _from_row_tiles(tiles_ref, BM).astype(BF16)
        h = jax.nn.silu(_dot(xb, wgb_ref[...])) * _dot(xb, wub_ref[...])
        gate = jnp.transpose(jnp.broadcast_to(rg_ref[0], (LANES, BM)))
        y = _dot(h.astype(BF16), wdb_ref[...])
        for j in range(8):
            tiles_ref[pl.ds(j, BM, stride=8), :] = y[:, j * LANES:(j + 1) * LANES] * gate
        y_ref[...] = _pack_tiles(tiles_ref[...])

    @pl.when(i >= nreal)
    def _():
        y_ref[...] = _pack_tiles(jnp.zeros((BM * 8, LANES), F32))


def _experts(x_packed, row_tok, row_gate, block_e, nreal, w_gate, w_up, w_down, layer):
    _, E, D, H = w_gate.shape
    NB = block_e.shape[0]
    BM = EXPERT_ROWS
    P = PACKED_SUBLANES
    rt3 = row_tok.reshape(NB, 1, BM)
    grid_spec = pltpu.PrefetchScalarGridSpec(
        num_scalar_prefetch=2,
        grid=(NB,),
        in_specs=[pl.BlockSpec((1, 1, BM), lambda i, be, nr: (i, 0, 0), memory_space=pltpu.SMEM),
                  pl.BlockSpec((1, 1, BM), lambda i, be, nr: (i, 0, 0)),
                  pl.BlockSpec(memory_space=pl.ANY),
                  pl.BlockSpec((None, 1, D, H), lambda i, be, nr: (layer, be[i], 0, 0)),
                  pl.BlockSpec((None, 1, D, H), lambda i, be, nr: (layer, be[i], 0, 0)),
                  pl.BlockSpec((None, 1, H, D), lambda i, be, nr: (layer, be[i], 0, 0))],
        out_specs=pl.BlockSpec((BM * P, LANES), lambda i, be, nr: (i, 0)),
        scratch_shapes=[pltpu.VMEM(x_packed.shape, U32), pltpu.VMEM((BM * P, LANES), U32),
                        pltpu.VMEM((BM * 8, LANES), F32), pltpu.VMEM((D, H), BF16), pltpu.VMEM((D, H), BF16), pltpu.VMEM((H, D), BF16),
                        pltpu.SemaphoreType.DMA((1,))],
    )
    return pl.pallas_call(
        _experts_body,
        grid_spec=grid_spec,
        out_shape=jax.ShapeDtypeStruct((NB * BM * P, LANES), U32),
        compiler_params=pltpu.CompilerParams(dimension_semantics=("arbitrary",),
                                             vmem_limit_bytes=EXPERTS_VMEM_LIMIT_BYTES),
        name="experts",
    )(block_e, nreal, rt3, row_gate.reshape(NB, 1, BM), x_packed, w_gate, w_up, w_down)


def _row_gather_start(offset_of, n_rows, src_hbm, dst, sem):
    P = PACKED_SUBLANES

    def body(p, carry):
        for u in range(2):
            r = 2 * p + u
            pltpu.make_async_copy(src_hbm.at[pl.ds(pl.multiple_of(offset_of(r), P), P), :],
                                  dst.at[pl.ds(pl.multiple_of(r * P, P), P), :], sem).start(priority=u)
        return carry
    lax.fori_loop(0, n_rows // 2, body, 0, unroll=GATHER_UNROLL // 2)


def _row_gather_wait(src_hbm, dst, sem):
    pltpu.make_async_copy(src_hbm.at[pl.ds(0, dst.shape[0]), :], dst, sem).wait()


def _combine_body(d_cur_ref, d_nxt_ref, x_ref, y_hbm, wsg_ref, wsu_ref, wsd_ref, lng_ref, lnb_ref,
                  o_ref, buf_ref, tiles_ref, sem):
    i = pl.program_id(0)
    nt = pl.num_programs(0)
    slot = i % 2
    tm = COMBINE_TILE

    def start(d_ref, s):
        _row_gather_start(lambda r: d_ref[0, 0, r], TOP_K * tm, y_hbm, buf_ref.at[s], sem.at[s])

    @pl.when(i == 0)
    def _():
        start(d_cur_ref, 0)

    @pl.when(i + 1 < nt)
    def _():
        start(d_nxt_ref, 1 - slot)

    x = x_ref[...]
    xb = x.astype(BF16)
    hs = jax.nn.silu(_dot(xb, wsg_ref[...])) * _dot(xb, wsu_ref[...])
    acc = _dot(hs.astype(BF16), wsd_ref[...])
    _row_gather_wait(y_hbm, buf_ref.at[slot], sem.at[slot])
    rows = tm * PACKED_SUBLANES
    mix = _unpack_tiles(buf_ref[slot, 0:rows, :])
    for k in range(1, TOP_K):
        mix = mix + _unpack_tiles(buf_ref[slot, k * rows:(k + 1) * rows, :])
    tiles_ref[...] = mix
    routed = _from_row_tiles(tiles_ref, tm)
    o_ref[...] = _standardize(DEEPNORM_ALPHA * x + (routed + acc)) * lng_ref[...] + lnb_ref[...]


def _combine_ln(x2d, y_tiles, dest_tiles, wsg, wsu, wsd, g_row, beta_row):
    T, D = x2d.shape
    H = wsg.shape[1]
    tm = COMBINE_TILE
    nt = T // tm
    row = pl.BlockSpec((1, D), lambda i: (0, 0))
    return pl.pallas_call(
        _combine_body,
        grid=(nt,),
        in_specs=[pl.BlockSpec((1, 1, TOP_K * tm), lambda i: (i, 0, 0), memory_space=pltpu.SMEM),
                  pl.BlockSpec((1, 1, TOP_K * tm), lambda i: (jnp.minimum(i + 1, nt - 1), 0, 0),
                               memory_space=pltpu.SMEM),
                  pl.BlockSpec((tm, D), lambda i: (i, 0)),
                  pl.BlockSpec(memory_space=pl.ANY),
                  pl.BlockSpec((D, H), lambda i: (0, 0)),
                  pl.BlockSpec((D, H), lambda i: (0, 0)),
                  pl.BlockSpec((H, D), lambda i: (0, 0)), row, row],
        out_specs=pl.BlockSpec((tm, D), lambda i: (i, 0)),
        out_shape=jax.ShapeDtypeStruct((T, D), F32),
        scratch_shapes=[pltpu.VMEM((2, TOP_K * tm * PACKED_SUBLANES, LANES), U32),
                        pltpu.VMEM((tm * 8, LANES), F32), pltpu.SemaphoreType.DMA((2,))],
        compiler_params=_cparams("arbitrary"),
        name="combine_ln",
    )(dest_tiles, dest_tiles, x2d, y_tiles, wsg, wsu, wsd, g_row, beta_row)


def _moe_ln(x2d, x_tiles, w_router, b_router, w_gate, w_up, w_down, ws_gate, ws_up, ws_down, g_row, beta_row,
            layer):
    T, D = x2d.shape
    E = w_router.shape[1]
    BM = EXPERT_ROWS
    NB = (T * TOP_K) // BM + E
    idx_t, gate_t, pos_t, cnt = _router(x2d, jnp.transpose(w_router), b_router.reshape(E, 1))
    counts = cnt[:, 0].astype(I32)
    padded = (counts + BM - 1) // BM * BM
    pend = jnp.cumsum(padded)
    pstart = pend - padded
    nreal = (pend[-1] // BM).astype(I32).reshape(1)
    block_start = jnp.arange(NB, dtype=I32) * BM
    block_e = jnp.minimum(jnp.sum((pend[None, :] <= block_start[:, None]).astype(I32), axis=1), E - 1)
    eids = jnp.arange(E, dtype=I32)
    dest_t = jnp.sum(jnp.where(idx_t[:, :, None] == eids, pstart, 0), axis=-1) + pos_t
    span = 2 * T
    tok = jnp.arange(T, dtype=I32)
    real_keys = (idx_t * span + tok[None, :]).reshape(-1)
    j = jnp.arange(BM, dtype=I32)
    pad_keys = jnp.where(j[None, :] < (padded - counts)[:, None], eids[:, None] * span + T + j[None, :],
                         jnp.iinfo(jnp.int32).max).reshape(-1)
    pad_gates = jnp.zeros((E * BM,), F32)
    keys, row_gate = lax.sort((jnp.concatenate([real_keys, pad_keys]),
                               jnp.concatenate([gate_t.reshape(-1), pad_gates])), num_keys=1)
    low = keys % span
    row_tok = jnp.where((keys < E * span) & (low < T), low, 0) * PACKED_SUBLANES
    y_tiles = _experts(x_tiles, row_tok, row_gate, block_e, nreal, w_gate, w_up, w_down, layer)
    nt = T // COMBINE_TILE
    dest_tiles = jnp.transpose((dest_t * PACKED_SUBLANES).reshape(TOP_K, nt, COMBINE_TILE),
                               (1, 0, 2)).reshape(nt, 1, -1)
    return _combine_ln(x2d, y_tiles, dest_tiles, ws_gate.astype(BF16), ws_up.astype(BF16), ws_down.astype(BF16),
                       g_row, beta_row)


def _position_tables(S):
    pos = jnp.arange(S, dtype=F32)[:, None]
    attn_freq = ROPE_THETA ** (-jnp.arange(0, A_HEAD_DIM, 2, dtype=F32) / A_HEAD_DIM)
    ang = pos * attn_freq[None, :]
    cosa = jnp.tile(jnp.cos(ang), (1, 4))
    sina = jnp.tile(jnp.concatenate([-jnp.sin(ang), jnp.sin(ang)], axis=1), (1, 2))
    ret_freq = ROPE_THETA ** (-jnp.linspace(0.0, 1.0, R_KEY_DIM // 2, dtype=F32))
    angr = pos * ret_freq[None, :]
    cosr = jnp.tile(jnp.cos(angr), (1, 2))
    sinr = jnp.concatenate([-jnp.sin(angr), jnp.sin(angr)], axis=1)
    C = CHUNK
    log_g = jnp.log1p(-jnp.exp2(-5.0 - jnp.arange(R_HEADS, dtype=F32)))
    idx = jnp.arange(C, dtype=F32)
    diff = idx[:, None] - idx[None, :]
    decay = jnp.where(diff >= 0, jnp.exp(log_g[:, None, None] * jnp.maximum(diff, 0.0)), 0.0)
    kdec = jnp.exp(log_g[:, None] * (C - 1.0 - idx)[None, :])
    qdec = jnp.exp(log_g[:, None] * (idx + 1.0)[None, :])
    bc = lambda t: jnp.broadcast_to(t[:, :, None], (R_HEADS, C, R_KEY_DIM))
    gch = jnp.exp(log_g * C)
    return cosa, sina, cosr, sinr, decay, bc(qdec), bc(kdec), gch


def kernel(x, ab_w_in, ab_b_in, ab_sinks, ab_w_out, ab_b_out, cd_w_in, cd_b_in, sgu_w, sgu_b, conv_w, conv_b,
           lru_wa, lru_ba, lru_wx, lru_bx, lru_lambda, cd_w_out, cd_b_out, ln_mix_g, ln_mix_b, router_w, router_b,
           exp_w_gate, exp_w_up, exp_w_down, shared_w_gate, shared_w_up, shared_w_down, ln_ffn_g, ln_ffn_b):
    B, S, D = x.shape
    T = B * S
    tables = _position_tables(S)
    row = lambda v: v.reshape(1, -1)
    x2d = x.reshape(T, D)
    for layer in range(DEPTH):
        j = layer // 2
        if layer % 2 == 0:
            z = _inproj(x2d, ab_w_in[j].astype(BF16), row(ab_b_in[j]))
            y = _ab_core(z.reshape(B, S, -1), ab_sinks[j], tables)
            w_out, b_out = ab_w_out[j], ab_b_out[j]
        else:
            z = _inproj(x2d, cd_w_in[j].astype(BF16), row(cd_b_in[j]))
            y = _cd_core(z.reshape(B, S, -1), sgu_w[j], jnp.transpose(sgu_b[j]), conv_w[j], row(conv_b[j]),
                         lru_wa[j].astype(BF16), row(lru_ba[j]), lru_wx[j].astype(BF16), row(lru_bx[j]),
                         row(lru_lambda[j]))
            w_out, b_out = cd_w_out[j], cd_b_out[j]
        x2d, x_tiles = _outproj_ln(y.reshape(T, -1), w_out.astype(BF16), row(b_out), x2d,
                                   row(ln_mix_g[layer]), row(ln_mix_b[layer]))
        x2d = _moe_ln(x2d, x_tiles, router_w[layer], router_b[layer], exp_w_gate, exp_w_up, exp_w_down,
                      shared_w_gate[layer], shared_w_up[layer], shared_w_down[layer],
                      row(ln_ffn_g[layer]), row(ln_ffn_b[layer]), layer)
    return x2d.reshape(B, S, D)
```

```python
import functools
import math

import jax
import jax.numpy as jnp
from jax import lax
from jax.experimental import pallas as pl
from jax.experimental.pallas import tpu as pltpu

F32 = jnp.float32
BF16 = jnp.bfloat16
I32 = jnp.int32
U32 = jnp.uint32

CHUNK = 128
A_HEADS, A_KV_HEADS, A_HEAD_DIM = 16, 2, 64
R_HEADS, R_KEY_DIM, R_VAL_DIM = 4, 128, 256
SGU_GROUPS, SGU_GROUP_DIM = 4, 256
LRU_HEADS, LRU_BLOCK = 5, 256
CONV_WIDTH = 4
LRU_C = 8.0
TOP_K = 8
ROUTE_SCALE = 2.5
ROPE_THETA = 10000.0
LN_EPS = 1e-5
NEG_INF = -1e30
DEPTH = 4
DEEPNORM_ALPHA = (2 * DEPTH) ** 0.25

LANES = 128
PACKED_SUBLANES = 4
VMEM_LIMIT_BYTES = 48 * 1024 * 1024
EXPERTS_VMEM_LIMIT_BYTES = 56 * 1024 * 1024
ROW_TILE = 256
ROUTER_TILE = 512
EXPERT_ROWS = 512
COMBINE_TILE = 256
GATHER_UNROLL = 32


def _cparams(*sem):
    return pltpu.CompilerParams(dimension_semantics=sem, vmem_limit_bytes=VMEM_LIMIT_BYTES)


def _standardize(x):
    xc = x - jnp.mean(x, axis=-1, keepdims=True)
    return xc * lax.rsqrt(jnp.mean(xc * xc, axis=-1, keepdims=True) + LN_EPS)


def _dot(a, b):
    return jnp.dot(a, b, preferred_element_type=F32)


def _dot_nt(a, b):
    return lax.dot_general(a, b, (((1,), (1,)), ((), ())), preferred_element_type=F32)


def _inproj_body(x_ref, w_ref, b_ref, o_ref):
    o_ref[...] = _dot(x_ref[...].astype(BF16), w_ref[...]) + b_ref[...]


def _inproj(x2d, w_bf16, b_row):
    T, D = x2d.shape
    N = w_bf16.shape[1]
    return pl.pallas_call(
        _inproj_body,
        grid=(T // ROW_TILE,),
        in_specs=[pl.BlockSpec((ROW_TILE, D), lambda i: (i, 0)),
                  pl.BlockSpec((D, N), lambda i: (0, 0)),
                  pl.BlockSpec((1, N), lambda i: (0, 0))],
        out_specs=pl.BlockSpec((ROW_TILE, N), lambda i: (i, 0)),
        out_shape=jax.ShapeDtypeStruct((T, N), F32),
        compiler_params=_cparams("parallel"),
        name="inproj",
    )(x2d, w_bf16, b_row)


def _ab_body(sinks_ref, gch_ref, z_ref, cosa_ref, sina_ref, cosr_ref, sinr_ref, decay_ref, qdec_ref, kdec_ref,
             o_ref, kprev_ref, vprev_ref, state_ref):
    n = pl.program_id(1)
    C = CHUNK

    @pl.when(n == 0)
    def _():
        kprev_ref[...] = jnp.zeros_like(kprev_ref)
        vprev_ref[...] = jnp.zeros_like(vprev_ref)
        state_ref[...] = jnp.zeros_like(state_ref)

    lane = lax.broadcasted_iota(I32, (C, LANES), 1)
    lo_half = lane < 64
    first_32 = (lane & 32) == 0
    cosa = cosa_ref[...]
    sina = sina_ref[...]

    def rope64(x):
        swapped = jnp.where(first_32, pltpu.roll(x, 96, 1), pltpu.roll(x, 32, 1))
        return x * cosa + swapped * sina

    k = rope64(z_ref[0, :, 1024:1152])
    v = z_ref[0, :, 1152:1280]
    kw = jnp.concatenate([kprev_ref[...], k], axis=0)
    vw = jnp.concatenate([vprev_ref[...], v], axis=0)
    kprev_ref[...] = k
    vprev_ref[...] = v
    kw_sw = pltpu.roll(kw, 64, 1)
    vw_sw = pltpu.roll(vw, 64, 1)
    lane2 = lax.broadcasted_iota(I32, (2 * C, LANES), 1)
    lo2 = lane2 < 64

    rows = 8 * C
    qi = lax.broadcasted_iota(I32, (rows, 2 * C), 0) & (C - 1)
    kj = lax.broadcasted_iota(I32, (rows, 2 * C), 1)
    valid = (kj > qi) & (kj <= qi + C) & ((n > 0) | (kj >= C))
    rblk = lax.broadcasted_iota(I32, (rows, 1), 0) // C

    for g in range(A_KV_HEADS):
        if g == 0:
            kdup = jnp.where(lo2, kw, kw_sw)
            vlo = jnp.where(lo2, vw, 0.0)
            vhi = jnp.where(lo2, 0.0, vw_sw)
        else:
            kdup = jnp.where(lo2, kw_sw, kw)
            vlo = jnp.where(lo2, vw_sw, 0.0)
            vhi = jnp.where(lo2, 0.0, vw)
        qs = [rope64(z_ref[0, :, c * LANES:(c + 1) * LANES]) * (A_HEAD_DIM ** -0.5)
              for c in range(4 * g, 4 * g + 4)]
        qg = jnp.concatenate([jnp.where(lo_half, qc, 0.0) for qc in qs]
                             + [jnp.where(lo_half, 0.0, qc) for qc in qs], axis=0).astype(BF16)
        s = _dot_nt(qg, kdup.astype(BF16))
        s = jnp.where(valid, s, NEG_INF)
        sink = jnp.zeros((rows, 1), F32)
        for j in range(8):
            head = 8 * g + (2 * j if j < 4 else 2 * (j - 4) + 1)
            sink = jnp.where(rblk == j, sinks_ref[head], sink)
        m = jnp.maximum(jnp.max(s, axis=-1, keepdims=True), sink)
        e = jnp.exp(s - m)
        den = jnp.sum(e, axis=-1, keepdims=True) + jnp.exp(sink - m)
        p = (e * (1.0 / den)).astype(BF16)
        p2 = jnp.concatenate([p[:4 * C], p[4 * C:]], axis=1)
        vblk = jnp.concatenate([vlo, vhi], axis=0).astype(BF16)
        o = _dot(p2, vblk)
        for c in range(4):
            col = (4 * g + c) * LANES
            o_ref[0, :, col:col + LANES] = o[c * C:(c + 1) * C].astype(o_ref.dtype)

    cosr = cosr_ref[...]
    sinr = sinr_ref[...]

    def rope128(x):
        return x * cosr + pltpu.roll(x, 64, 1) * sinr

    q0, k0, v0, g0 = 1280, 1792, 2304, 3328
    for h in range(R_HEADS):
        qh = rope128(z_ref[0, :, q0 + h * 128:q0 + (h + 1) * 128])
        kh = rope128(z_ref[0, :, k0 + h * 128:k0 + (h + 1) * 128]) * (R_KEY_DIM ** -0.5)
        vb = z_ref[0, :, v0 + h * 256:v0 + (h + 1) * 256].astype(BF16)
        gh = z_ref[0, :, g0 + h * 256:g0 + (h + 1) * 256]
        s = _dot_nt(qh.astype(BF16), kh.astype(BF16)) * decay_ref[h]
        st = state_ref[h]
        o = _dot(s.astype(BF16), vb) + _dot((qh * qdec_ref[h]).astype(BF16), st.astype(BF16))
        kd_t = jnp.transpose(kh * kdec_ref[h]).astype(BF16)
        state_ref[h] = gch_ref[h] * st + _dot(kd_t, vb)
        y = jax.nn.silu(gh) * _standardize(o)
        col = 1024 + h * 256
        o_ref[0, :, col:col + 256] = y.astype(o_ref.dtype)


def _ab_core(z, sinks, tables):
    B, S, W = z.shape
    cosa, sina, cosr, sinr, decay, qdec, kdec, gch = tables
    smem = pl.BlockSpec(memory_space=pltpu.SMEM)
    tab = pl.BlockSpec((CHUNK, LANES), lambda b, n: (n, 0))
    const3 = pl.BlockSpec((R_HEADS, CHUNK, CHUNK), lambda b, n: (0, 0, 0))
    return pl.pallas_call(
        _ab_body,
        grid=(B, S // CHUNK),
        in_specs=[smem, smem,
                  pl.BlockSpec((1, CHUNK, W), lambda b, n: (b, n, 0)),
                  tab, tab, tab, tab, const3, const3, const3],
        out_specs=pl.BlockSpec((1, CHUNK, 2048), lambda b, n: (b, n, 0)),
        out_shape=jax.ShapeDtypeStruct((B, S, 2048), BF16),
        scratch_shapes=[pltpu.VMEM((CHUNK, LANES), F32), pltpu.VMEM((CHUNK, LANES), F32),
                        pltpu.VMEM((R_HEADS, R_KEY_DIM, R_VAL_DIM), F32)],
        compiler_params=_cparams("parallel", "arbitrary"),
        name="ab_core",
    )(sinks, gch, z, cosa, sina, cosr, sinr, decay, qdec, kdec)


def _cd_body(z_ref, sw_ref, sbt_ref, cw_ref, cb_ref, wa_ref, ba_ref, wx_ref, bx_ref, lam_ref,
             o_ref, xwin_ref, hc_ref):
    n = pl.program_id(1)
    C = CHUNK
    WL = LRU_HEADS * LRU_BLOCK

    @pl.when(n == 0)
    def _():
        xwin_ref[0:8, :] = jnp.zeros((8, WL), F32)
        hc_ref[...] = jnp.zeros_like(hc_ref)

    gu = jax.nn.gelu(z_ref[0, :, 0:1024])
    vn = _standardize(jax.nn.gelu(z_ref[0, :, 1024:2048])).astype(BF16)
    ri = lax.broadcasted_iota(I32, (C, C), 0)
    ci = lax.broadcasted_iota(I32, (C, C), 1)
    for g in range(SGU_GROUPS):
        w = jnp.where(ci <= ri, sw_ref[g], 0.0).astype(BF16)
        col = g * SGU_GROUP_DIM
        mixed = _dot(w, vn[:, col:col + SGU_GROUP_DIM]) + sbt_ref[:, g:g + 1]
        o_ref[0, :, col:col + SGU_GROUP_DIM] = (gu[:, col:col + SGU_GROUP_DIM] * mixed).astype(o_ref.dtype)

    xr = z_ref[0, :, 3328:3328 + WL]
    xwin_ref[8:8 + C, :] = xr
    xc = (cw_ref[3:4, :] * xr + cw_ref[2:3, :] * xwin_ref[7:7 + C, :] + cw_ref[1:2, :] * xwin_ref[6:6 + C, :]
          + cw_ref[0:1, :] * xwin_ref[5:5 + C, :] + cb_ref[...])
    xwin_ref[0:8, :] = xr[C - 8:C]
    xcb = xc.astype(BF16)
    ga, gi = [], []
    for h in range(LRU_HEADS):
        xb = xcb[:, h * LRU_BLOCK:(h + 1) * LRU_BLOCK]
        ga.append(_dot(xb, wa_ref[h]))
        gi.append(_dot(xb, wx_ref[h]))
    gate_r = jax.nn.sigmoid(jnp.concatenate(ga, axis=1) + ba_ref[...])
    gate_i = jax.nn.sigmoid(jnp.concatenate(gi, axis=1) + bx_ref[...])
    nl = -lam_ref[...]
    softplus = jnp.maximum(nl, 0.0) + jnp.log1p(jnp.exp(-jnp.abs(nl)))
    log_a = -LRU_C * gate_r * softplus
    a = jnp.exp(log_a)
    th = jnp.tanh(log_a)
    bv = jnp.sqrt(-2.0 * th / (1.0 - th)) * (gate_i * xc)
    r8 = lax.broadcasted_iota(I32, (C, WL), 0) & 7
    for d in (1, 2, 4):
        keep = r8 >= d
        a_sh = jnp.where(keep, pltpu.roll(a, d, 0), 1.0)
        b_sh = jnp.where(keep, pltpu.roll(bv, d, 0), 0.0)
        bv = a * b_sh + bv
        a = a * a_sh
    carry = hc_ref[...]
    hs = []
    for grp in range(C // 8):
        hg = bv[grp * 8:(grp + 1) * 8] + a[grp * 8:(grp + 1) * 8] * carry
        carry = hg[7:8]
        hs.append(hg)
    hc_ref[...] = carry
    hseq = jnp.concatenate(hs, axis=0)
    y = jax.nn.gelu(z_ref[0, :, 2048:2048 + WL]) * hseq
    o_ref[0, :, 1024:1024 + WL] = y.astype(o_ref.dtype)


def _cd_core(z, sgu_w, sgu_bt, conv_w, conv_b, wa, ba, wx, bx, lam):
    B, S, W = z.shape
    WL = LRU_HEADS * LRU_BLOCK
    full2 = lambda r, c: pl.BlockSpec((r, c), lambda b, n: (0, 0))
    full3 = lambda a, r, c: pl.BlockSpec((a, r, c), lambda b, n: (0, 0, 0))
    return pl.pallas_call(
        _cd_body,
        grid=(B, S // CHUNK),
        in_specs=[pl.BlockSpec((1, CHUNK, W), lambda b, n: (b, n, 0)),
                  full3(SGU_GROUPS, CHUNK, CHUNK), full2(CHUNK, SGU_GROUPS),
                  full2(CONV_WIDTH, WL), full2(1, WL),
                  full3(LRU_HEADS, LRU_BLOCK, LRU_BLOCK), full2(1, WL),
                  full3(LRU_HEADS, LRU_BLOCK, LRU_BLOCK), full2(1, WL), full2(1, WL)],
        out_specs=pl.BlockSpec((1, CHUNK, 1024 + WL), lambda b, n: (b, n, 0)),
        out_shape=jax.ShapeDtypeStruct((B, S, 1024 + WL), BF16),
        scratch_shapes=[pltpu.VMEM((8 + CHUNK, WL), F32), pltpu.VMEM((1, WL), F32)],
        compiler_params=_cparams("parallel", "arbitrary"),
        name="cd_core",
    )(z, sgu_w, sgu_bt, conv_w, conv_b, wa, ba, wx, bx, lam)


def _to_row_tiles(tiles_ref, val):
    n = val.shape[0]
    for j in range(val.shape[1] // LANES):
        tiles_ref[pl.ds(j, n, stride=8), :] = val[:, j * LANES:(j + 1) * LANES]


def _from_row_tiles(tiles_ref, n):
    return jnp.concatenate([tiles_ref[pl.ds(j, n, stride=8), :] for j in range(8)], axis=1)


def _pack_tiles(tiles):
    return pltpu.bitcast(tiles.astype(BF16), U32)


def _unpack_tiles(words):
    return pltpu.bitcast(words, BF16).astype(F32)


def _outproj_body(y_ref, w_ref, b_ref, x_ref, g_ref, beta_ref, o_ref, opacked_ref, tiles_ref):
    h = _dot(y_ref[...], w_ref[...]) + b_ref[...]
    xn = _standardize(DEEPNORM_ALPHA * x_ref[...] + h) * g_ref[...] + beta_ref[...]
    o_ref[...] = xn
    _to_row_tiles(tiles_ref, xn)
    opacked_ref[...] = _pack_tiles(tiles_ref[...])


def _outproj_ln(y2d, w_bf16, b_row, x2d, g_row, beta_row):
    T, Ky = y2d.shape
    D = x2d.shape[1]
    row = pl.BlockSpec((1, D), lambda i: (0, 0))
    return pl.pallas_call(
        _outproj_body,
        grid=(T // ROW_TILE,),
        in_specs=[pl.BlockSpec((ROW_TILE, Ky), lambda i: (i, 0)),
                  pl.BlockSpec((Ky, D), lambda i: (0, 0)), row,
                  pl.BlockSpec((ROW_TILE, D), lambda i: (i, 0)), row, row],
        out_specs=[pl.BlockSpec((ROW_TILE, D), lambda i: (i, 0)),
                   pl.BlockSpec((ROW_TILE * PACKED_SUBLANES, LANES), lambda i: (i, 0))],
        out_shape=[jax.ShapeDtypeStruct((T, D), F32), jax.ShapeDtypeStruct((T * PACKED_SUBLANES, LANES), U32)],
        scratch_shapes=[pltpu.VMEM((ROW_TILE * 8, LANES), F32)],
        compiler_params=_cparams("parallel"),
        name="outproj_ln",
    )(y2d, w_bf16, b_row, x2d, g_row, beta_row)


def _router_body(x_ref, wrt_ref, bcol_ref, idx_ref, gate_ref, pos_ref, cnt_ref, carry_ref):
    i = pl.program_id(0)
    E, tm = carry_ref.shape

    @pl.when(i == 0)
    def _():
        carry_ref[...] = jnp.zeros_like(carry_ref)

    logits = lax.dot_general(wrt_ref[...], x_ref[...], (((1,), (1,)), ((), ())),
                             precision=lax.Precision.HIGHEST, preferred_element_type=F32)
    scores = jax.nn.sigmoid(logits)
    sel = scores + bcol_ref[...]
    eio = lax.broadcasted_iota(I32, (E, tm), 0)
    onehots, idxs, tops = [], [], []
    for _ in range(TOP_K):
        m = jnp.max(sel, axis=0, keepdims=True)
        idx = jnp.min(jnp.where(sel == m, eio, E), axis=0, keepdims=True)
        hit = eio == idx
        tops.append(jnp.sum(jnp.where(hit, scores, 0.0), axis=0, keepdims=True))
        sel = jnp.where(hit, -jnp.inf, sel)
        onehots.append(hit)
        idxs.append(idx)
    chosen = onehots[0]
    for hit in onehots[1:]:
        chosen = chosen | hit
    mask = jnp.where(chosen, 1.0, 0.0).astype(BF16)
    ti = lax.broadcasted_iota(I32, (tm, tm), 0)
    tj = lax.broadcasted_iota(I32, (tm, tm), 1)
    earlier = jnp.where(ti < tj, 1.0, 0.0).astype(BF16)
    before = _dot(mask, earlier) + carry_ref[...]
    total = _dot(mask, jnp.ones((tm, tm), BF16))
    carry_ref[...] = carry_ref[...] + total
    cnt_ref[...] = carry_ref[:, 0:LANES]
    tsum = tops[0]
    for t in tops[1:]:
        tsum = tsum + t
    inv = ROUTE_SCALE / tsum
    gate_ref[...] = jnp.concatenate([t * inv for t in tops], axis=0)
    idx_ref[...] = jnp.concatenate(idxs, axis=0)
    pos_ref[...] = jnp.concatenate(
        [jnp.sum(jnp.where(hit, before, 0.0), axis=0, keepdims=True) for hit in onehots], axis=0).astype(I32)


def _router(x2d, wrt, bcol):
    T, D = x2d.shape
    E = wrt.shape[0]
    tm = ROUTER_TILE
    out8 = lambda dt: jax.ShapeDtypeStruct((TOP_K, T), dt)
    spec8 = pl.BlockSpec((TOP_K, tm), lambda i: (0, i))
    return pl.pallas_call(
        _router_body,
        grid=(T // tm,),
        in_specs=[pl.BlockSpec((tm, D), lambda i: (i, 0)),
                  pl.BlockSpec((E, D), lambda i: (0, 0)),
                  pl.BlockSpec((E, 1), lambda i: (0, 0))],
        out_specs=[spec8, spec8, spec8, pl.BlockSpec((E, LANES), lambda i: (0, 0))],
        out_shape=[out8(I32), out8(F32), out8(I32), jax.ShapeDtypeStruct((E, LANES), F32)],
        scratch_shapes=[pltpu.VMEM((E, tm), F32)],
        compiler_params=_cparams("arbitrary"),
        name="router",
    )(x2d, wrt, bcol)


def _experts_body(be_ref, nr_ref, rt_ref, rg_ref, x_hbm, wg_ref, wu_ref, wd_ref, y_ref,
                  xres_ref, stage_ref, tiles_ref, wgb_ref, wub_ref, wdb_ref, sem):
    i = pl.program_id(0)
    nreal = nr_ref[0]
    BM = EXPERT_ROWS
    P = PACKED_SUBLANES

    @pl.when(i == 0)
    def _():
        load_all = pltpu.make_async_copy(x_hbm, xres_ref, sem.at[0])
        load_all.start()
        load_all.wait()

    @pl.when(i < nreal)
    def _():
        @pl.when((i == 0) | (be_ref[i] != be_ref[jnp.maximum(i - 1, 0)]))
        def _():
            wgb_ref[...] = wg_ref[0].astype(BF16)
            wub_ref[...] = wu_ref[0].astype(BF16)
            wdb_ref[...] = wd_ref[0].astype(BF16)

        def copy_row(r, carry):
            src = pl.multiple_of(rt_ref[0, 0, r], P)
            stage_ref[pl.ds(pl.multiple_of(r * P, P), P), :] = xres_ref[pl.ds(src, P), :]
            return carry
        lax.fori_loop(0, BM, copy_row, 0, unroll=GATHER_UNROLL)

        tiles_ref[...] = _unpack_tiles(stage_ref[...])
        xb = _from_row_tiles(tiles_ref, BM).astype(BF16)
        h = jax.nn.silu(_dot(xb, wgb_ref[...])) * _dot(xb, wub_ref[...])
        gate = jnp.transpose(jnp.broadcast_to(rg_ref[0], (LANES, BM)))
        y = _dot(h.astype(BF16), wdb_ref[...])
        for j in range(8):
            tiles_ref[pl.ds(j, BM, stride=8), :] = y[:, j * LANES:(j + 1) * LANES] * gate
        y_ref[...] = _pack_tiles(tiles_ref[...])

    @pl.when(i >= nreal)
    def _():
        y_ref[...] = _pack_tiles(jnp.zeros((BM * 8, LANES), F32))


def _experts(x_packed, row_tok, row_gate, block_e, nreal, w_gate, w_up, w_down, layer):
    _, E, D, H = w_gate.shape
    NB = block_e.shape[0]
    BM = EXPERT_ROWS
    P = PACKED_SUBLANES
    rt3 = row_tok.reshape(NB, 1, BM)
    grid_spec = pltpu.PrefetchScalarGridSpec(
        num_scalar_prefetch=2,
        grid=(NB,),
        in_specs=[pl.BlockSpec((1, 1, BM), lambda i, be, nr: (i, 0, 0), memory_space=pltpu.SMEM),
                  pl.BlockSpec((1, 1, BM), lambda i, be, nr: (i, 0, 0)),
                  pl.BlockSpec(memory_space=pl.ANY),
                  pl.BlockSpec((None, 1, D, H), lambda i, be, nr: (layer, be[i], 0, 0)),
                  pl.BlockSpec((None, 1, D, H), lambda i, be, nr: (layer, be[i], 0, 0)),
                  pl.BlockSpec((None, 1, H, D), lambda i, be, nr: (layer, be[i], 0, 0))],
        out_specs=pl.BlockSpec((BM * P, LANES), lambda i, be, nr: (i, 0)),
        scratch_shapes=[pltpu.VMEM(x_packed.shape, U32), pltpu.VMEM((BM * P, LANES), U32),
                        pltpu.VMEM((BM * 8, LANES), F32), pltpu.VMEM((D, H), BF16), pltpu.VMEM((D, H), BF16), pltpu.VMEM((H, D), BF16),
                        pltpu.SemaphoreType.DMA((1,))],
    )
    return pl.pallas_call(
        _experts_body,
        grid_spec=grid_spec,
        out_shape=jax.ShapeDtypeStruct((NB * BM * P, LANES), U32),
        compiler_params=pltpu.CompilerParams(dimension_semantics=("arbitrary",),
                                             vmem_limit_bytes=EXPERTS_VMEM_LIMIT_BYTES),
        name="experts",
    )(block_e, nreal, rt3, row_gate.reshape(NB, 1, BM), x_packed, w_gate, w_up, w_down)


def _row_gather_start(offset_of, n_rows, src_hbm, dst, sem):
    P = PACKED_SUBLANES

    def body(p, carry):
        for u in range(2):
            r = 2 * p + u
            pltpu.make_async_copy(src_hbm.at[pl.ds(pl.multiple_of(offset_of(r), P), P), :],
                                  dst.at[pl.ds(pl.multiple_of(r * P, P), P), :], sem).start(priority=u)
        return carry
    lax.fori_loop(0, n_rows // 2, body, 0, unroll=GATHER_UNROLL // 2)


def _row_gather_wait(src_hbm, dst, sem):
    pltpu.make_async_copy(src_hbm.at[pl.ds(0, dst.shape[0]), :], dst, sem).wait()


def _combine_body(d_cur_ref, d_nxt_ref, x_ref, y_hbm, wsg_ref, wsu_ref, wsd_ref, lng_ref, lnb_ref,
                  o_ref, buf_ref, tiles_ref, sem):
    i = pl.program_id(0)
    nt = pl.num_programs(0)
    slot = i % 2
    tm = COMBINE_TILE

    def start(d_ref, s):
        _row_gather_start(lambda r: d_ref[0, 0, r], TOP_K * tm, y_hbm, buf_ref.at[s], sem.at[s])

    @pl.when(i == 0)
    def _():
        start(d_cur_ref, 0)

    @pl.when(i + 1 < nt)
    def _():
        start(d_nxt_ref, 1 - slot)

    x = x_ref[...]
    xb = x.astype(BF16)
    hs = jax.nn.silu(_dot(xb, wsg_ref[...])) * _dot(xb, wsu_ref[...])
    acc = _dot(hs.astype(BF16), wsd_ref[...])
    _row_gather_wait(y_hbm, buf_ref.at[slot], sem.at[slot])
    rows = tm * PACKED_SUBLANES
    mix = _unpack_tiles(buf_ref[slot, 0:rows, :])
    for k in range(1, TOP_K):
        mix = mix + _unpack_tiles(buf_ref[slot, k * rows:(k + 1) * rows, :])
    tiles_ref[...] = mix
    routed = _from_row_tiles(tiles_ref, tm)
    o_ref[...] = _standardize(DEEPNORM_ALPHA * x + (routed + acc)) * lng_ref[...] + lnb_ref[...]


def _combine_ln(x2d, y_tiles, dest_tiles, wsg, wsu, wsd, g_row, beta_row):
    T, D = x2d.shape
    H = wsg.shape[1]
    tm = COMBINE_TILE
    nt = T // tm
    row = pl.BlockSpec((1, D), lambda i: (0, 0))
    return pl.pallas_call(
        _combine_body,
        grid=(nt,),
        in_specs=[pl.BlockSpec((1, 1, TOP_K * tm), lambda i: (i, 0, 0), memory_space=pltpu.SMEM),
                  pl.BlockSpec((1, 1, TOP_K * tm), lambda i: (jnp.minimum(i + 1, nt - 1), 0, 0),
                               memory_space=pltpu.SMEM),
                  pl.BlockSpec((tm, D), lambda i: (i, 0)),
                  pl.BlockSpec(memory_space=pl.ANY),
                  pl.BlockSpec((D, H), lambda i: (0, 0)),
                  pl.BlockSpec((D, H), lambda i: (0, 0)),
                  pl.BlockSpec((H, D), lambda i: (0, 0)), row, row],
        out_specs=pl.BlockSpec((tm, D), lambda i: (i, 0)),
        out_shape=jax.ShapeDtypeStruct((T, D), F32),
        scratch_shapes=[pltpu.VMEM((2, TOP_K * tm * PACKED_SUBLANES, LANES), U32),
                        pltpu.VMEM((tm * 8, LANES), F32), pltpu.SemaphoreType.DMA((2,))],
        compiler_params=_cparams("arbitrary"),
        name="combine_ln",
    )(dest_tiles, dest_tiles, x2d, y_tiles, wsg, wsu, wsd, g_row, beta_row)


def _moe_ln(x2d, x_tiles, w_router, b_router, w_gate, w_up, w_down, ws_gate, ws_up, ws_down, g_row, beta_row,
            layer):
    T, D = x2d.shape
    E = w_router.shape[1]
    BM = EXPERT_ROWS
    NB = (T * TOP_K) // BM + E
    idx_t, gate_t, pos_t, cnt = _router(x2d, jnp.transpose(w_router), b_router.reshape(E, 1))
    counts = cnt[:, 0].astype(I32)
    padded = (counts + BM - 1) // BM * BM
    pend = jnp.cumsum(padded)
    pstart = pend - padded
    nreal = (pend[-1] // BM).astype(I32).reshape(1)
    block_start = jnp.arange(NB, dtype=I32) * BM
    block_e = jnp.minimum(jnp.sum((pend[None, :] <= block_start[:, None]).astype(I32), axis=1), E - 1)
    eids = jnp.arange(E, dtype=I32)
    dest_t = jnp.sum(jnp.where(idx_t[:, :, None] == eids, pstart, 0), axis=-1) + pos_t
    span = 2 * T
    tok = jnp.arange(T, dtype=I32)
    real_keys = (idx_t * span + tok[None, :]).reshape(-1)
    j = jnp.arange(BM, dtype=I32)
    pad_keys = jnp.where(j[None, :] < (padded - counts)[:, None], eids[:, None] * span + T + j[None, :],
                         jnp.iinfo(jnp.int32).max).reshape(-1)
    pad_gates = jnp.zeros((E * BM,), F32)
    keys, row_gate = lax.sort((jnp.concatenate([real_keys, pad_keys]),
                               jnp.concatenate([gate_t.reshape(-1), pad_gates])), num_keys=1)
    low = keys % span
    row_tok = jnp.where((keys < E * span) & (low < T), low, 0) * PACKED_SUBLANES
    y_tiles = _experts(x_tiles, row_tok, row_gate, block_e, nreal, w_gate, w_up, w_down, layer)
    nt = T // COMBINE_TILE
    dest_tiles = jnp.transpose((dest_t * PACKED_SUBLANES).reshape(TOP_K, nt, COMBINE_TILE),
                               (1, 0, 2)).reshape(nt, 1, -1)
    return _combine_ln(x2d, y_tiles, dest_tiles, ws_gate.astype(BF16), ws_up.astype(BF16), ws_down.astype(BF16),
                       g_row, beta_row)


def _position_tables(S):
    pos = jnp.arange(S, dtype=F32)[:, None]
    attn_freq = ROPE_THETA ** (-jnp.arange(0, A_HEAD_DIM, 2, dtype=F32) / A_HEAD_DIM)
    ang = pos * attn_freq[None, :]
    cosa = jnp.tile(jnp.cos(ang), (1, 4))
    sina = jnp.tile(jnp.concatenate([-jnp.sin(ang), jnp.sin(ang)], axis=1), (1, 2))
    ret_freq = ROPE_THETA ** (-jnp.linspace(0.0, 1.0, R_KEY_DIM // 2, dtype=F32))
    angr = pos * ret_freq[None, :]
    cosr = jnp.tile(jnp.cos(angr), (1, 2))
    sinr = jnp.concatenate([-jnp.sin(angr), jnp.sin(angr)], axis=1)
    C = CHUNK
    log_g = jnp.log1p(-jnp.exp2(-5.0 - jnp.arange(R_HEADS, dtype=F32)))
    idx = jnp.arange(C, dtype=F32)
    diff = idx[:, None] - idx[None, :]
    decay = jnp.where(diff >= 0, jnp.exp(log_g[:, None, None] * jnp.maximum(diff, 0.0)), 0.0)
    kdec = jnp.exp(log_g[:, None] * (C - 1.0 - idx)[None, :])
    qdec = jnp.exp(log_g[:, None] * (idx + 1.0)[None, :])
    bc = lambda t: jnp.broadcast_to(t[:, :, None], (R_HEADS, C, R_KEY_DIM))
    gch = jnp.exp(log_g * C)
    return cosa, sina, cosr, sinr, decay, bc(qdec), bc(kdec), gch


def kernel(x, ab_w_in, ab_b_in, ab_sinks, ab_w_out, ab_b_out, cd_w_in, cd_b_in, sgu_w, sgu_b, conv_w, conv_b,
           lru_wa, lru_ba, lru_wx, lru_bx, lru_lambda, cd_w_out, cd_b_out, ln_mix_g, ln_mix_b, router_w, router_b,
           exp_w_gate, exp_w_up, exp_w_down, shared_w_gate, shared_w_up, shared_w_down, ln_ffn_g, ln_ffn_b):
    B, S, D = x.shape
    T = B * S
    tables = _position_tables(S)
    row = lambda v: v.reshape(1, -1)
    x2d = x.reshape(T, D)
    for layer in range(DEPTH):
        j = layer // 2
        if layer % 2 == 0:
            z = _inproj(x2d, ab_w_in[j].astype(BF16), row(ab_b_in[j]))
            y = _ab_core(z.reshape(B, S, -1), ab_sinks[j], tables)
            w_out, b_out = ab_w_out[j], ab_b_out[j]
        else:
            z = _inproj(x2d, cd_w_in[j].astype(BF16), row(cd_b_in[j]))
            y = _cd_core(z.reshape(B, S, -1), sgu_w[j], jnp.transpose(sgu_b[j]), conv_w[j], row(conv_b[j]),
                         lru_wa[j].astype(BF16), row(lru_ba[j]), lru_wx[j].astype(BF16), row(lru_bx[j]),
                         row(lru_lambda[j]))
            w_out, b_out = cd_w_out[j], cd_b_out[j]
        x2d, x_tiles = _outproj_ln(y.reshape(T, -1), w_out.astype(BF16), row(b_out), x2d,
                                   row(ln_mix_g[layer]), row(ln_mix_b[layer]))
        x2d = _moe_ln(x2d, x_tiles, router_w[layer], router_b[layer], exp_w_gate, exp_w_up, exp_w_down,
                      shared_w_gate[layer], shared_w_up[layer], shared_w_down[layer],
                      row(ln_ffn_g[layer]), row(ln_ffn_b[layer]), layer)
    return x2d.reshape(B, S, D)
```

```python
import functools
import math

import jax
import jax.numpy as jnp
from jax import lax
from jax.experimental import pallas as pl
from jax.experimental.pallas import tpu as pltpu

F32 = jnp.float32
BF16 = jnp.bfloat16
I32 = jnp.int32
U32 = jnp.uint32

CHUNK = 128
A_HEADS, A_KV_HEADS, A_HEAD_DIM = 16, 2, 64
R_HEADS, R_KEY_DIM, R_VAL_DIM = 4, 128, 256
SGU_GROUPS, SGU_GROUP_DIM = 4, 256
LRU_HEADS, LRU_BLOCK = 5, 256
CONV_WIDTH = 4
LRU_C = 8.0
TOP_K = 8
ROUTE_SCALE = 2.5
ROPE_THETA = 10000.0
LN_EPS = 1e-5
NEG_INF = -1e30
DEPTH = 4
DEEPNORM_ALPHA = (2 * DEPTH) ** 0.25

LANES = 128
PACKED_SUBLANES = 4
VMEM_LIMIT_BYTES = 48 * 1024 * 1024
EXPERTS_VMEM_LIMIT_BYTES = 56 * 1024 * 1024
ROW_TILE = 256
ROUTER_TILE = 512
EXPERT_ROWS = 512
COMBINE_TILE = 256
GATHER_UNROLL = 32


def _cparams(*sem):
    return pltpu.CompilerParams(dimension_semantics=sem, vmem_limit_bytes=VMEM_LIMIT_BYTES)


def _standardize(x):
    xc = x - jnp.mean(x, axis=-1, keepdims=True)
    return xc * lax.rsqrt(jnp.mean(xc * xc, axis=-1, keepdims=True) + LN_EPS)


def _gelu(x):
    c = math.sqrt(2.0 / math.pi)
    half = 0.5 * x
    return half + half * jnp.tanh(x * (c + (c * 0.044715) * (x * x)))


def _dot(a, b):
    return jnp.dot(a, b, preferred_element_type=F32)


def _dot_nt(a, b):
    return lax.dot_general(a, b, (((1,), (1,)), ((), ())), preferred_element_type=F32)


def _inproj_body(x_ref, w_ref, b_ref, o_ref):
    o_ref[...] = _dot(x_ref[...].astype(BF16), w_ref[...]) + b_ref[...]


def _inproj(x2d, w_bf16, b_row):
    T, D = x2d.shape
    N = w_bf16.shape[1]
    return pl.pallas_call(
        _inproj_body,
        grid=(T // ROW_TILE,),
        in_specs=[pl.BlockSpec((ROW_TILE, D), lambda i: (i, 0)),
                  pl.BlockSpec((D, N), lambda i: (0, 0)),
                  pl.BlockSpec((1, N), lambda i: (0, 0))],
        out_specs=pl.BlockSpec((ROW_TILE, N), lambda i: (i, 0)),
        out_shape=jax.ShapeDtypeStruct((T, N), F32),
        compiler_params=_cparams("parallel"),
        name="inproj",
    )(x2d, w_bf16, b_row)


def _ab_body(sinks_ref, gch_ref, z_ref, cosa_ref, sina_ref, cosr_ref, sinr_ref, decay_ref, qdec_ref, kdec_ref,
             o_ref, kprev_ref, vprev_ref, state_ref):
    n = pl.program_id(1)
    C = CHUNK

    @pl.when(n == 0)
    def _():
        kprev_ref[...] = jnp.zeros_like(kprev_ref)
        vprev_ref[...] = jnp.zeros_like(vprev_ref)
        state_ref[...] = jnp.zeros_like(state_ref)

    lane = lax.broadcasted_iota(I32, (C, LANES), 1)
    lo_half = lane < 64
    first_32 = (lane & 32) == 0
    cosa = cosa_ref[...]
    sina = sina_ref[...]

    def rope64(x):
        swapped = jnp.where(first_32, pltpu.roll(x, 96, 1), pltpu.roll(x, 32, 1))
        return x * cosa + swapped * sina

    k = rope64(z_ref[0, :, 1024:1152])
    v = z_ref[0, :, 1152:1280]
    kw = jnp.concatenate([kprev_ref[...], k], axis=0)
    vw = jnp.concatenate([vprev_ref[...], v], axis=0)
    kprev_ref[...] = k
    vprev_ref[...] = v
    kw_sw = pltpu.roll(kw, 64, 1)
    vw_sw = pltpu.roll(vw, 64, 1)
    lane2 = lax.broadcasted_iota(I32, (2 * C, LANES), 1)
    lo2 = lane2 < 64

    rows = 8 * C
    qi = lax.broadcasted_iota(I32, (rows, 2 * C), 0) & (C - 1)
    kj = lax.broadcasted_iota(I32, (rows, 2 * C), 1)
    valid = (kj > qi) & (kj <= qi + C) & ((n > 0) | (kj >= C))
    rblk = lax.broadcasted_iota(I32, (rows, 1), 0) // C

    for g in range(A_KV_HEADS):
        if g == 0:
            kdup = jnp.where(lo2, kw, kw_sw)
            vlo = jnp.where(lo2, vw, 0.0)
            vhi = jnp.where(lo2, 0.0, vw_sw)
        else:
            kdup = jnp.where(lo2, kw_sw, kw)
            vlo = jnp.where(lo2, vw_sw, 0.0)
            vhi = jnp.where(lo2, 0.0, vw)
        qs = [rope64(z_ref[0, :, c * LANES:(c + 1) * LANES]) * (A_HEAD_DIM ** -0.5)
              for c in range(4 * g, 4 * g + 4)]
        qg = jnp.concatenate([jnp.where(lo_half, qc, 0.0) for qc in qs]
                             + [jnp.where(lo_half, 0.0, qc) for qc in qs], axis=0).astype(BF16)
        s = _dot_nt(qg, kdup.astype(BF16))
        s = jnp.where(valid, s, NEG_INF)
        sink = jnp.zeros((rows, 1), F32)
        for j in range(8):
            head = 8 * g + (2 * j if j < 4 else 2 * (j - 4) + 1)
            sink = jnp.where(rblk == j, sinks_ref[head], sink)
        m = jnp.maximum(jnp.max(s, axis=-1, keepdims=True), sink)
        e = jnp.exp(s - m)
        den = jnp.sum(e, axis=-1, keepdims=True) + jnp.exp(sink - m)
        p = (e * (1.0 / den)).astype(BF16)
        p2 = jnp.concatenate([p[:4 * C], p[4 * C:]], axis=1)
        vblk = jnp.concatenate([vlo, vhi], axis=0).astype(BF16)
        o = _dot(p2, vblk)
        for c in range(4):
            col = (4 * g + c) * LANES
            o_ref[0, :, col:col + LANES] = o[c * C:(c + 1) * C].astype(o_ref.dtype)

    cosr = cosr_ref[...]
    sinr = sinr_ref[...]

    def rope128(x):
        return x * cosr + pltpu.roll(x, 64, 1) * sinr

    q0, k0, v0, g0 = 1280, 1792, 2304, 3328
    for h in range(R_HEADS):
        qh = rope128(z_ref[0, :, q0 + h * 128:q0 + (h + 1) * 128])
        kh = rope128(z_ref[0, :, k0 + h * 128:k0 + (h + 1) * 128]) * (R_KEY_DIM ** -0.5)
        vb = z_ref[0, :, v0 + h * 256:v0 + (h + 1) * 256].astype(BF16)
        gh = z_ref[0, :, g0 + h * 256:g0 + (h + 1) * 256]
        s = _dot_nt(qh.astype(BF16), kh.astype(BF16)) * decay_ref[h]
        st = state_ref[h]
        o = _dot(s.astype(BF16), vb) + _dot((qh * qdec_ref[h]).astype(BF16), st.astype(BF16))
        kd_t = jnp.transpose(kh * kdec_ref[h]).astype(BF16)
        state_ref[h] = gch_ref[h] * st + _dot(kd_t, vb)
        y = jax.nn.silu(gh) * _standardize(o)
        col = 1024 + h * 256
        o_ref[0, :, col:col + 256] = y.astype(o_ref.dtype)


def _ab_core(z, sinks, tables):
    B, S, W = z.shape
    cosa, sina, cosr, sinr, decay, qdec, kdec, gch = tables
    smem = pl.BlockSpec(memory_space=pltpu.SMEM)
    tab = pl.BlockSpec((CHUNK, LANES), lambda b, n: (n, 0))
    const3 = pl.BlockSpec((R_HEADS, CHUNK, CHUNK), lambda b, n: (0, 0, 0))
    return pl.pallas_call(
        _ab_body,
        grid=(B, S // CHUNK),
        in_specs=[smem, smem,
                  pl.BlockSpec((1, CHUNK, W), lambda b, n: (b, n, 0)),
                  tab, tab, tab, tab, const3, const3, const3],
        out_specs=pl.BlockSpec((1, CHUNK, 2048), lambda b, n: (b, n, 0)),
        out_shape=jax.ShapeDtypeStruct((B, S, 2048), BF16),
        scratch_shapes=[pltpu.VMEM((CHUNK, LANES), F32), pltpu.VMEM((CHUNK, LANES), F32),
                        pltpu.VMEM((R_HEADS, R_KEY_DIM, R_VAL_DIM), F32)],
        compiler_params=_cparams("parallel", "arbitrary"),
        name="ab_core",
    )(sinks, gch, z, cosa, sina, cosr, sinr, decay, qdec, kdec)


def _cd_body(z_ref, sw_ref, sbt_ref, cw_ref, cb_ref, wa_ref, ba_ref, wx_ref, bx_ref, lam_ref,
             o_ref, xwin_ref, hc_ref):
    n = pl.program_id(1)
    C = CHUNK
    WL = LRU_HEADS * LRU_BLOCK

    @pl.when(n == 0)
    def _():
        xwin_ref[0:8, :] = jnp.zeros((8, WL), F32)
        hc_ref[...] = jnp.zeros_like(hc_ref)

    gu = _gelu(z_ref[0, :, 0:1024])
    vn = _standardize(_gelu(z_ref[0, :, 1024:2048])).astype(BF16)
    ri = lax.broadcasted_iota(I32, (C, C), 0)
    ci = lax.broadcasted_iota(I32, (C, C), 1)
    for g in range(SGU_GROUPS):
        w = jnp.where(ci <= ri, sw_ref[g], 0.0).astype(BF16)
        col = g * SGU_GROUP_DIM
        mixed = _dot(w, vn[:, col:col + SGU_GROUP_DIM]) + sbt_ref[:, g:g + 1]
        o_ref[0, :, col:col + SGU_GROUP_DIM] = (gu[:, col:col + SGU_GROUP_DIM] * mixed).astype(o_ref.dtype)

    xr = z_ref[0, :, 3328:3328 + WL]
    xwin_ref[8:8 + C, :] = xr
    xc = (cw_ref[3:4, :] * xr + cw_ref[2:3, :] * xwin_ref[7:7 + C, :] + cw_ref[1:2, :] * xwin_ref[6:6 + C, :]
          + cw_ref[0:1, :] * xwin_ref[5:5 + C, :] + cb_ref[...])
    xwin_ref[0:8, :] = xr[C - 8:C]
    xcb = xc.astype(BF16)
    ga, gi = [], []
    for h in range(LRU_HEADS):
        xb = xcb[:, h * LRU_BLOCK:(h + 1) * LRU_BLOCK]
        ga.append(_dot(xb, wa_ref[h]))
        gi.append(_dot(xb, wx_ref[h]))
    gate_r = jax.nn.sigmoid(jnp.concatenate(ga, axis=1) + ba_ref[...])
    gate_i = jax.nn.sigmoid(jnp.concatenate(gi, axis=1) + bx_ref[...])
    nl = -lam_ref[...]
    softplus = jnp.maximum(nl, 0.0) + jnp.log1p(jnp.exp(-jnp.abs(nl)))
    log_a = -LRU_C * gate_r * softplus
    a = jnp.exp(log_a)
    th = jnp.tanh(log_a)
    bv = jnp.sqrt(-2.0 * th / (1.0 - th)) * (gate_i * xc)
    r8 = lax.broadcasted_iota(I32, (C, WL), 0) & 7
    for d in (1, 2, 4):
        keep = r8 >= d
        a_sh = jnp.where(keep, pltpu.roll(a, d, 0), 1.0)
        b_sh = jnp.where(keep, pltpu.roll(bv, d, 0), 0.0)
        bv = a * b_sh + bv
        a = a * a_sh
    carry = hc_ref[...]
    hs = []
    for grp in range(C // 8):
        hg = bv[grp * 8:(grp + 1) * 8] + a[grp * 8:(grp + 1) * 8] * carry
        carry = hg[7:8]
        hs.append(hg)
    hc_ref[...] = carry
    hseq = jnp.concatenate(hs, axis=0)
    y = _gelu(z_ref[0, :, 2048:2048 + WL]) * hseq
    o_ref[0, :, 1024:1024 + WL] = y.astype(o_ref.dtype)


def _cd_core(z, sgu_w, sgu_bt, conv_w, conv_b, wa, ba, wx, bx, lam):
    B, S, W = z.shape
    WL = LRU_HEADS * LRU_BLOCK
    full2 = lambda r, c: pl.BlockSpec((r, c), lambda b, n: (0, 0))
    full3 = lambda a, r, c: pl.BlockSpec((a, r, c), lambda b, n: (0, 0, 0))
    return pl.pallas_call(
        _cd_body,
        grid=(B, S // CHUNK),
        in_specs=[pl.BlockSpec((1, CHUNK, W), lambda b, n: (b, n, 0)),
                  full3(SGU_GROUPS, CHUNK, CHUNK), full2(CHUNK, SGU_GROUPS),
                  full2(CONV_WIDTH, WL), full2(1, WL),
                  full3(LRU_HEADS, LRU_BLOCK, LRU_BLOCK), full2(1, WL),
                  full3(LRU_HEADS, LRU_BLOCK, LRU_BLOCK), full2(1, WL), full2(1, WL)],
        out_specs=pl.BlockSpec((1, CHUNK, 1024 + WL), lambda b, n: (b, n, 0)),
        out_shape=jax.ShapeDtypeStruct((B, S, 1024 + WL), BF16),
        scratch_shapes=[pltpu.VMEM((8 + CHUNK, WL), F32), pltpu.VMEM((1, WL), F32)],
        compiler_params=_cparams("parallel", "arbitrary"),
        name="cd_core",
    )(z, sgu_w, sgu_bt, conv_w, conv_b, wa, ba, wx, bx, lam)


def _to_row_tiles(tiles_ref, val):
    n = val.shape[0]
    for j in range(val.shape[1] // LANES):
        tiles_ref[pl.ds(j, n, stride=8), :] = val[:, j * LANES:(j + 1) * LANES]


def _from_row_tiles(tiles_ref, n):
    return jnp.concatenate([tiles_ref[pl.ds(j, n, stride=8), :] for j in range(8)], axis=1)


def _pack_tiles(tiles):
    return pltpu.bitcast(tiles.astype(BF16), U32)


def _unpack_tiles(words):
    return pltpu.bitcast(words, BF16).astype(F32)


def _outproj_body(y_ref, w_ref, b_ref, x_ref, g_ref, beta_ref, o_ref, opacked_ref, tiles_ref):
    h = _dot(y_ref[...], w_ref[...]) + b_ref[...]
    xn = _standardize(DEEPNORM_ALPHA * x_ref[...] + h) * g_ref[...] + beta_ref[...]
    o_ref[...] = xn
    _to_row_tiles(tiles_ref, xn)
    opacked_ref[...] = _pack_tiles(tiles_ref[...])


def _outproj_ln(y2d, w_bf16, b_row, x2d, g_row, beta_row):
    T, Ky = y2d.shape
    D = x2d.shape[1]
    row = pl.BlockSpec((1, D), lambda i: (0, 0))
    return pl.pallas_call(
        _outproj_body,
        grid=(T // ROW_TILE,),
        in_specs=[pl.BlockSpec((ROW_TILE, Ky), lambda i: (i, 0)),
                  pl.BlockSpec((Ky, D), lambda i: (0, 0)), row,
                  pl.BlockSpec((ROW_TILE, D), lambda i: (i, 0)), row, row],
        out_specs=[pl.BlockSpec((ROW_TILE, D), lambda i: (i, 0)),
                   pl.BlockSpec((ROW_TILE * PACKED_SUBLANES, LANES), lambda i: (i, 0))],
        out_shape=[jax.ShapeDtypeStruct((T, D), F32), jax.ShapeDtypeStruct((T * PACKED_SUBLANES, LANES), U32)],
        scratch_shapes=[pltpu.VMEM((ROW_TILE * 8, LANES), F32)],
        compiler_params=_cparams("parallel"),
        name="outproj_ln",
    )(y2d, w_bf16, b_row, x2d, g_row, beta_row)


def _router_body(x_ref, wrt_ref, bcol_ref, idx_ref, gate_ref, pos_ref, cnt_ref, carry_ref):
    i = pl.program_id(0)
    E, tm = carry_ref.shape

    @pl.when(i == 0)
    def _():
        carry_ref[...] = jnp.zeros_like(carry_ref)

    logits = lax.dot_general(wrt_ref[...], x_ref[...], (((1,), (1,)), ((), ())),
                             precision=lax.Precision.HIGHEST, preferred_element_type=F32)
    scores = jax.nn.sigmoid(logits)
    sel = scores + bcol_ref[...]
    eio = lax.broadcasted_iota(I32, (E, tm), 0)
    onehots, idxs, tops = [], [], []
    for _ in range(TOP_K):
        m = jnp.max(sel, axis=0, keepdims=True)
        idx = jnp.min(jnp.where(sel == m, eio, E), axis=0, keepdims=True)
        hit = eio == idx
        tops.append(jnp.sum(jnp.where(hit, scores, 0.0), axis=0, keepdims=True))
        sel = jnp.where(hit, -jnp.inf, sel)
        onehots.append(hit)
        idxs.append(idx)
    chosen = onehots[0]
    for hit in onehots[1:]:
        chosen = chosen | hit
    mask = jnp.where(chosen, 1.0, 0.0).astype(BF16)
    ti = lax.broadcasted_iota(I32, (tm, tm), 0)
    tj = lax.broadcasted_iota(I32, (tm, tm), 1)
    earlier = jnp.where(ti < tj, 1.0, 0.0).astype(BF16)
    before = _dot(mask, earlier) + carry_ref[...]
    total = _dot(mask, jnp.ones((tm, tm), BF16))
    carry_ref[...] = carry_ref[...] + total
    cnt_ref[...] = carry_ref[:, 0:LANES]
    tsum = tops[0]
    for t in tops[1:]:
        tsum = tsum + t
    inv = ROUTE_SCALE / tsum
    gate_ref[...] = jnp.concatenate([t * inv for t in tops], axis=0)
    idx_ref[...] = jnp.concatenate(idxs, axis=0)
    pos_ref[...] = jnp.concatenate(
        [jnp.sum(jnp.where(hit, before, 0.0), axis=0, keepdims=True) for hit in onehots], axis=0).astype(I32)


def _router(x2d, wrt, bcol):
    T, D = x2d.shape
    E = wrt.shape[0]
    tm = ROUTER_TILE
    out8 = lambda dt: jax.ShapeDtypeStruct((TOP_K, T), dt)
    spec8 = pl.BlockSpec((TOP_K, tm), lambda i: (0, i))
    return pl.pallas_call(
        _router_body,
        grid=(T // tm,),
        in_specs=[pl.BlockSpec((tm, D), lambda i: (i, 0)),
                  pl.BlockSpec((E, D), lambda i: (0, 0)),
                  pl.BlockSpec((E, 1), lambda i: (0, 0))],
        out_specs=[spec8, spec8, spec8, pl.BlockSpec((E, LANES), lambda i: (0, 0))],
        out_shape=[out8(I32), out8(F32), out8(I32), jax.ShapeDtypeStruct((E, LANES), F32)],
        scratch_shapes=[pltpu.VMEM((E, tm), F32)],
        compiler_params=_cparams("arbitrary"),
        name="router",
    )(x2d, wrt, bcol)


def _experts_body(be_ref, nr_ref, rt_ref, rg_ref, x_hbm, wg_ref, wu_ref, wd_ref, y_ref,
                  xres_ref, stage_ref, tiles_ref, wgb_ref, wub_ref, wdb_ref, sem):
    i = pl.program_id(0)
    nreal = nr_ref[0]
    BM = EXPERT_ROWS
    P = PACKED_SUBLANES

    @pl.when(i == 0)
    def _():
        load_all = pltpu.make_async_copy(x_hbm, xres_ref, sem.at[0])
        load_all.start()
        load_all.wait()

    @pl.when(i < nreal)
    def _():
        @pl.when((i == 0) | (be_ref[i] != be_ref[jnp.maximum(i - 1, 0)]))
        def _():
            wgb_ref[...] = wg_ref[0].astype(BF16)
            wub_ref[...] = wu_ref[0].astype(BF16)
            wdb_ref[...] = wd_ref[0].astype(BF16)

        def copy_row(r, carry):
            src = pl.multiple_of(rt_ref[0, 0, r], P)
            stage_ref[pl.ds(pl.multiple_of(r * P, P), P), :] = xres_ref[pl.ds(src, P), :]
            return carry
        lax.fori_loop(0, BM, copy_row, 0, unroll=GATHER_UNROLL)

        tiles_ref[...] = _unpack_tiles(stage_ref[...])
        xb = _from_row_tiles(tiles_ref, BM).astype(BF16)
        h = jax.nn.silu(_dot(xb, wgb_ref[...])) * _dot(xb, wub_ref[...])
        gate = jnp.transpose(jnp.broadcast_to(rg_ref[0], (LANES, BM)))
        y = _dot(h.astype(BF16), wdb_ref[...])
        for j in range(8):
            tiles_ref[pl.ds(j, BM, stride=8), :] = y[:, j * LANES:(j + 1) * LANES] * gate
        y_ref[...] = _pack_tiles(tiles_ref[...])

    @pl.when(i >= nreal)
    def _():
        y_ref[...] = _pack_tiles(jnp.zeros((BM * 8, LANES), F32))


def _experts(x_packed, row_tok, row_gate, block_e, nreal, w_gate, w_up, w_down, layer):
    _, E, D, H = w_gate.shape
    NB = block_e.shape[0]
    BM = EXPERT_ROWS
    P = PACKED_SUBLANES
    rt3 = row_tok.reshape(NB, 1, BM)
    grid_spec = pltpu.PrefetchScalarGridSpec(
        num_scalar_prefetch=2,
        grid=(NB,),
        in_specs=[pl.BlockSpec((1, 1, BM), lambda i, be, nr: (i, 0, 0), memory_space=pltpu.SMEM),
                  pl.BlockSpec((1, 1, BM), lambda i, be, nr: (i, 0, 0)),
                  pl.BlockSpec(memory_space=pl.ANY),
                  pl.BlockSpec((None, 1, D, H), lambda i, be, nr: (layer, be[i], 0, 0)),
                  pl.BlockSpec((None, 1, D, H), lambda i, be, nr: (layer, be[i], 0, 0)),
                  pl.BlockSpec((None, 1, H, D), lambda i, be, nr: (layer, be[i], 0, 0))],
        out_specs=pl.BlockSpec((BM * P, LANES), lambda i, be, nr: (i, 0)),
        scratch_shapes=[pltpu.VMEM(x_packed.shape, U32), pltpu.VMEM((BM * P, LANES), U32),
                        pltpu.VMEM((BM * 8, LANES), F32), pltpu.VMEM((D, H), BF16), pltpu.VMEM((D, H), BF16), pltpu.VMEM((H, D), BF16),
                        pltpu.SemaphoreType.DMA((1,))],
    )
    return pl.pallas_call(
        _experts_body,
        grid_spec=grid_spec,
        out_shape=jax.ShapeDtypeStruct((NB * BM * P, LANES), U32),
        compiler_params=pltpu.CompilerParams(dimension_semantics=("arbitrary",),
                                             vmem_limit_bytes=EXPERTS_VMEM_LIMIT_BYTES),
        name="experts",
    )(block_e, nreal, rt3, row_gate.reshape(NB, 1, BM), x_packed, w_gate, w_up, w_down)


def _row_gather_start(offset_of, n_rows, src_hbm, dst, sem):
    P = PACKED_SUBLANES

    def body(p, carry):
        for u in range(2):
            r = 2 * p + u
            pltpu.make_async_copy(src_hbm.at[pl.ds(pl.multiple_of(offset_of(r), P), P), :],
                                  dst.at[pl.ds(pl.multiple_of(r * P, P), P), :], sem).start(priority=u)
        return carry
    lax.fori_loop(0, n_rows // 2, body, 0, unroll=GATHER_UNROLL // 2)


def _row_gather_wait(src_hbm, dst, sem):
    pltpu.make_async_copy(src_hbm.at[pl.ds(0, dst.shape[0]), :], dst, sem).wait()


def _combine_body(d_cur_ref, d_nxt_ref, x_ref, y_hbm, wsg_ref, wsu_ref, wsd_ref, lng_ref, lnb_ref,
                  o_ref, buf_ref, tiles_ref, sem):
    i = pl.program_id(0)
    nt = pl.num_programs(0)
    slot = i % 2
    tm = COMBINE_TILE

    def start(d_ref, s):
        _row_gather_start(lambda r: d_ref[0, 0, r], TOP_K * tm, y_hbm, buf_ref.at[s], sem.at[s])

    @pl.when(i == 0)
    def _():
        start(d_cur_ref, 0)

    @pl.when(i + 1 < nt)
    def _():
        start(d_nxt_ref, 1 - slot)

    x = x_ref[...]
    xb = x.astype(BF16)
    hs = jax.nn.silu(_dot(xb, wsg_ref[...])) * _dot(xb, wsu_ref[...])
    acc = _dot(hs.astype(BF16), wsd_ref[...])
    _row_gather_wait(y_hbm, buf_ref.at[slot], sem.at[slot])
    rows = tm * PACKED_SUBLANES
    mix = _unpack_tiles(buf_ref[slot, 0:rows, :])
    for k in range(1, TOP_K):
        mix = mix + _unpack_tiles(buf_ref[slot, k * rows:(k + 1) * rows, :])
    tiles_ref[...] = mix
    routed = _from_row_tiles(tiles_ref, tm)
    o_ref[...] = _standardize(DEEPNORM_ALPHA * x + (routed + acc)) * lng_ref[...] + lnb_ref[...]


def _combine_ln(x2d, y_tiles, dest_tiles, wsg, wsu, wsd, g_row, beta_row):
    T, D = x2d.shape
    H = wsg.shape[1]
    tm = COMBINE_TILE
    nt = T // tm
    row = pl.BlockSpec((1, D), lambda i: (0, 0))
    return pl.pallas_call(
        _combine_body,
        grid=(nt,),
        in_specs=[pl.BlockSpec((1, 1, TOP_K * tm), lambda i: (i, 0, 0), memory_space=pltpu.SMEM),
                  pl.BlockSpec((1, 1, TOP_K * tm), lambda i: (jnp.minimum(i + 1, nt - 1), 0, 0),
                               memory_space=pltpu.SMEM),
                  pl.BlockSpec((tm, D), lambda i: (i, 0)),
                  pl.BlockSpec(memory_space=pl.ANY),
                  pl.BlockSpec((D, H), lambda i: (0, 0)),
                  pl.BlockSpec((D, H), lambda i: (0, 0)),
                  pl.BlockSpec((H, D), lambda i: (0, 0)), row, row],
        out_specs=pl.BlockSpec((tm, D), lambda i: (i, 0)),
        out_shape=jax.ShapeDtypeStruct((T, D), F32),
        scratch_shapes=[pltpu.VMEM((2, TOP_K * tm * PACKED_SUBLANES, LANES), U32),
                        pltpu.VMEM((tm * 8, LANES), F32), pltpu.SemaphoreType.DMA((2,))],
        compiler_params=_cparams("arbitrary"),
        name="combine_ln",
    )(dest_tiles, dest_tiles, x2d, y_tiles, wsg, wsu, wsd, g_row, beta_row)


def _moe_ln(x2d, x_tiles, w_router, b_router, w_gate, w_up, w_down, ws_gate, ws_up, ws_down, g_row, beta_row,
            layer):
    T, D = x2d.shape
    E = w_router.shape[1]
    BM = EXPERT_ROWS
    NB = (T * TOP_K) // BM + E
    idx_t, gate_t, pos_t, cnt = _router(x2d, jnp.transpose(w_router), b_router.reshape(E, 1))
    counts = cnt[:, 0].astype(I32)
    padded = (counts + BM - 1) // BM * BM
    pend = jnp.cumsum(padded)
    pstart = pend - padded
    nreal = (pend[-1] // BM).astype(I32).reshape(1)
    block_start = jnp.arange(NB, dtype=I32) * BM
    block_e = jnp.minimum(jnp.sum((pend[None, :] <= block_start[:, None]).astype(I32), axis=1), E - 1)
    eids = jnp.arange(E, dtype=I32)
    dest_t = jnp.sum(jnp.where(idx_t[:, :, None] == eids, pstart, 0), axis=-1) + pos_t
    span = 2 * T
    tok = jnp.arange(T, dtype=I32)
    real_keys = (idx_t * span + tok[None, :]).reshape(-1)
    j = jnp.arange(BM, dtype=I32)
    pad_keys = jnp.where(j[None, :] < (padded - counts)[:, None], eids[:, None] * span + T + j[None, :],
                         jnp.iinfo(jnp.int32).max).reshape(-1)
    pad_gates = jnp.zeros((E * BM,), F32)
    keys, row_gate = lax.sort((jnp.concatenate([real_keys, pad_keys]),
                               jnp.concatenate([gate_t.reshape(-1), pad_gates])), num_keys=1)
    low = keys % span
    row_tok = jnp.where((keys < E * span) & (low < T), low, 0) * PACKED_SUBLANES
    y_tiles = _experts(x_tiles, row_tok, row_gate, block_e, nreal, w_gate, w_up, w_down, layer)
    nt = T // COMBINE_TILE
    dest_tiles = jnp.transpose((dest_t * PACKED_SUBLANES).reshape(TOP_K, nt, COMBINE_TILE),
                               (1, 0, 2)).reshape(nt, 1, -1)
    return _combine_ln(x2d, y_tiles, dest_tiles, ws_gate.astype(BF16), ws_up.astype(BF16), ws_down.astype(BF16),
                       g_row, beta_row)


def _position_tables(S):
    pos = jnp.arange(S, dtype=F32)[:, None]
    attn_freq = ROPE_THETA ** (-jnp.arange(0, A_HEAD_DIM, 2, dtype=F32) / A_HEAD_DIM)
    ang = pos * attn_freq[None, :]
    cosa = jnp.tile(jnp.cos(ang), (1, 4))
    sina = jnp.tile(jnp.concatenate([-jnp.sin(ang), jnp.sin(ang)], axis=1), (1, 2))
    ret_freq = ROPE_THETA ** (-jnp.linspace(0.0, 1.0, R_KEY_DIM // 2, dtype=F32))
    angr = pos * ret_freq[None, :]
    cosr = jnp.tile(jnp.cos(angr), (1, 2))
    sinr = jnp.concatenate([-jnp.sin(angr), jnp.sin(angr)], axis=1)
    C = CHUNK
    log_g = jnp.log1p(-jnp.exp2(-5.0 - jnp.arange(R_HEADS, dtype=F32)))
    idx = jnp.arange(C, dtype=F32)
    diff = idx[:, None] - idx[None, :]
    decay = jnp.where(diff >= 0, jnp.exp(log_g[:, None, None] * jnp.maximum(diff, 0.0)), 0.0)
    kdec = jnp.exp(log_g[:, None] * (C - 1.0 - idx)[None, :])
    qdec = jnp.exp(log_g[:, None] * (idx + 1.0)[None, :])
    bc = lambda t: jnp.broadcast_to(t[:, :, None], (R_HEADS, C, R_KEY_DIM))
    gch = jnp.exp(log_g * C)
    return cosa, sina, cosr, sinr, decay, bc(qdec), bc(kdec), gch


def kernel(x, ab_w_in, ab_b_in, ab_sinks, ab_w_out, ab_b_out, cd_w_in, cd_b_in, sgu_w, sgu_b, conv_w, conv_b,
           lru_wa, lru_ba, lru_wx, lru_bx, lru_lambda, cd_w_out, cd_b_out, ln_mix_g, ln_mix_b, router_w, router_b,
           exp_w_gate, exp_w_up, exp_w_down, shared_w_gate, shared_w_up, shared_w_down, ln_ffn_g, ln_ffn_b):
    B, S, D = x.shape
    T = B * S
    tables = _position_tables(S)
    row = lambda v: v.reshape(1, -1)
    x2d = x.reshape(T, D)
    for layer in range(DEPTH):
        j = layer // 2
        if layer % 2 == 0:
            z = _inproj(x2d, ab_w_in[j].astype(BF16), row(ab_b_in[j]))
            y = _ab_core(z.reshape(B, S, -1), ab_sinks[j], tables)
            w_out, b_out = ab_w_out[j], ab_b_out[j]
        else:
            z = _inproj(x2d, cd_w_in[j].astype(BF16), row(cd_b_in[j]))
            y = _cd_core(z.reshape(B, S, -1), sgu_w[j], jnp.transpose(sgu_b[j]), conv_w[j], row(conv_b[j]),
                         lru_wa[j].astype(BF16), row(lru_ba[j]), lru_wx[j].astype(BF16), row(lru_bx[j]),
                         row(lru_lambda[j]))
            w_out, b_out = cd_w_out[j], cd_b_out[j]
        x2d, x_tiles = _outproj_ln(y.reshape(T, -1), w_out.astype(BF16), row(b_out), x2d,
                                   row(ln_mix_g[layer]), row(ln_mix_b[layer]))
        x2d = _moe_ln(x2d, x_tiles, router_w[layer], router_b[layer], exp_w_gate, exp_w_up, exp_w_down,
                      shared_w_gate[layer], shared_w_up[layer], shared_w_down[layer],
                      row(ln_ffn_g[layer]), row(ln_ffn_b[layer]), layer)
    return x2d.reshape(B, S, D)
```
